```python
import jax
import jax.numpy as jnp
from jax import lax
import numpy as np

D_MODEL = 1024
BATCH = 16
SEQ = 2048
DEPTH = 1
DEC_BATCH = 128
DEC_SEQ = 4
PAST_LEN = 16384
PAGE_SIZE = 128

MLA_HEADS = 8
MLA_NOPE = 64
MLA_ROPE = 32
MLA_V = 64
Q_LORA = 768
KV_LORA = 256
MLA_SCALE = (MLA_NOPE + MLA_ROPE) ** -0.5
MLA_WIDTH = MLA_HEADS * MLA_V
SB_HEADS = 8
SB_DIM = 64
SB_SCALE = SB_DIM ** -0.5
SB_WIDTH = SB_HEADS * SB_DIM
MIX_WIDTH = MLA_WIDTH + SB_WIDTH
IN_WIDTH = Q_LORA + KV_LORA + MLA_ROPE + 3 * SB_WIDTH
IN_OFFSETS = (Q_LORA, Q_LORA + KV_LORA, Q_LORA + KV_LORA + MLA_ROPE,
              Q_LORA + KV_LORA + MLA_ROPE + SB_WIDTH,
              Q_LORA + KV_LORA + MLA_ROPE + 2 * SB_WIDTH)
MEM_TOKENS = 256
MEM_HEADS = 4
MEM_DIM = 64
MEM_WIDTH = MEM_HEADS * MEM_DIM
N_KEYS = 128
N_EXPERTS = N_KEYS * N_KEYS
PEER_HEADS = 8
PEER_KEY_DIM = 256
PEER_HALF = PEER_KEY_DIM // 2
PEER_TOPK = 16
PEER_BLOCK = 128
Q_BLOCK = 128
ROPE_THETA = 10000.0
EPS = 1e-6

kernel_name = 'hybrid_mla_stickbreak_peer_step'


def rmsnorm(x, g):
    xf = x.astype(jnp.float32)
    y = xf * lax.rsqrt(jnp.mean(xf * xf, axis=-1, keepdims=True) + EPS)
    return (y * g.astype(jnp.float32)).astype(x.dtype)


def rope(x, pos):
    half = x.shape[-1] // 2
    inv = ROPE_THETA ** (-jnp.arange(half, dtype=jnp.float32) / half)
    ang = pos.astype(jnp.float32)[:, None] * inv[None, :]
    cos = jnp.cos(ang)[:, None, :]
    sin = jnp.sin(ang)[:, None, :]
    xf = x.astype(jnp.float32)
    x1, x2 = xf[..., :half], xf[..., half:]
    return jnp.concatenate([x1 * cos - x2 * sin, x2 * cos + x1 * sin], axis=-1).astype(x.dtype)


def attention_inputs(h, pos, w_in, q_norm, w_uq, kv_norm, w_uk):
    b, t, _ = h.shape
    c_q, c_kv, k_r, q_s, k_s, v_s = jnp.split(h @ w_in, list(IN_OFFSETS), axis=-1)
    q = (rmsnorm(c_q, q_norm) @ w_uq).reshape(b, t, MLA_HEADS, MLA_NOPE + MLA_ROPE)
    q_lat = jnp.einsum('bthn,chn->bthc', q[..., :MLA_NOPE], w_uk)
    q_rope = rope(q[..., MLA_NOPE:], pos)
    c_kv = rmsnorm(c_kv, kv_norm)
    k_r = rope(k_r[:, :, None, :], pos)[:, :, 0, :]
    shp = (b, t, SB_HEADS, SB_DIM)
    return q_lat, q_rope, c_kv, k_r, q_s.reshape(shp), k_s.reshape(shp), v_s.reshape(shp)


def mla_core(q_lat, q_rope, q_pos, c_kv, k_r, k_pos):
    s = (jnp.einsum('bthc,bsc->bhts', q_lat, c_kv)
         + jnp.einsum('bthr,bsr->bhts', q_rope, k_r)).astype(jnp.float32) * MLA_SCALE
    s = jnp.where(k_pos[None, :] <= q_pos[:, None], s, -jnp.inf)
    p = jax.nn.softmax(s, axis=-1).astype(c_kv.dtype)
    return jnp.einsum('bhts,bsc->bthc', p, c_kv)


def sb_core(q, q_pos, k, v, k_pos):
    z = jnp.einsum('bthd,bshd->bhts', q, k).astype(jnp.float32) * SB_SCALE
    before = k_pos[None, :] < q_pos[:, None]
    log_keep = jnp.where(before, jax.nn.log_sigmoid(-z), 0.0)
    between = lax.cumsum(log_keep, axis=3, reverse=True) - log_keep
    a = jnp.where(before, jnp.exp(jax.nn.log_sigmoid(z) + between), 0.0)
    return jnp.einsum('bhts,bshd->bthd', a.astype(v.dtype), v)


def causal_blocks(core, qs, ks, q_pos, k_pos):
    n_q = q_pos.shape[0]
    n_past = k_pos.shape[0] - n_q
    outs = []
    for start in range(0, n_q, Q_BLOCK):
        end = min(start + Q_BLOCK, n_q)
        outs.append(core(*[a[:, start:end] for a in qs], q_pos[start:end],
                         *[a[:, :n_past + end] for a in ks], k_pos[:n_past + end]))
    return jnp.concatenate(outs, axis=1)


def token_mixers(h, q_pos, past, w_in, q_norm, w_uq, kv_norm, w_uk, w_uv, g_mla, g_sb, w_o):
    b, t, _ = h.shape
    q_lat, q_rope, c_kv, k_r, q_s, k_s, v_s = attention_inputs(h, q_pos, w_in, q_norm, w_uq, kv_norm, w_uk)
    new = (c_kv, k_r, k_s, v_s)
    if past is None:
        keys = new
    else:
        keys = tuple(jnp.concatenate([p, n], axis=1) for p, n in zip(past, new))
    k_pos = jnp.arange(keys[0].shape[1])
    o_lat = causal_blocks(mla_core, (q_lat, q_rope), keys[:2], q_pos, k_pos)
    o_sb = causal_blocks(sb_core, (q_s,), keys[2:], q_pos, k_pos)
    o_mla = jnp.einsum('bthc,chv->bthv', o_lat, w_uv).reshape(b, t, MLA_WIDTH)
    o = jnp.concatenate([rmsnorm(o_mla, g_mla), rmsnorm(o_sb.reshape(b, t, SB_WIDTH), g_sb)], axis=-1)
    return o @ w_o, new


def memory_kv(mem, mem_norm, w_mk, w_mv):
    b, m, _ = mem.shape
    mn = rmsnorm(mem, mem_norm)
    shp = (b, m, MEM_HEADS, MEM_DIM)
    return (mn @ w_mk).reshape(shp), (mn @ w_mv).reshape(shp)


def memory_attend(h, mem_k, mem_v, w_mq, w_mo):
    b, t, _ = h.shape
    q = (h @ w_mq).reshape(b, t, MEM_HEADS, MEM_DIM)
    s = jnp.einsum('bthd,bmhd->bhtm', q, mem_k).astype(jnp.float32) * MEM_DIM ** -0.5
    p = jax.nn.softmax(s, axis=-1).astype(mem_v.dtype)
    o = jnp.einsum('bhtm,bmhd->bthd', p, mem_v).reshape(b, t, MEM_WIDTH)
    return o @ w_mo


def peer(h, w_pq, sub_keys, u_tab, v_tab):
    lead = h.shape[:-1]
    x = h.reshape(-1, D_MODEL)
    n_tok = x.shape[0]
    n_blk = -(-n_tok // PEER_BLOCK)
    xp = jnp.pad(x, ((0, n_blk * PEER_BLOCK - n_tok), (0, 0))).reshape(n_blk, PEER_BLOCK, D_MODEL)

    def block(xb):
        q = (xb @ w_pq).reshape(PEER_BLOCK, PEER_HEADS, 2, PEER_HALF)
        s = jnp.einsum('thpd,hpnd->thpn', q, sub_keys).astype(jnp.float32)
        top_s, top_i = lax.top_k(s, PEER_TOPK)
        cand_s = top_s[:, :, 0, :, None] + top_s[:, :, 1, None, :]
        cand_i = top_i[:, :, 0, :, None] * N_KEYS + top_i[:, :, 1, None, :]
        cand_s = cand_s.reshape(PEER_BLOCK, PEER_HEADS, PEER_TOPK * PEER_TOPK)
        cand_i = cand_i.reshape(PEER_BLOCK, PEER_HEADS, PEER_TOPK * PEER_TOPK)
        best_s, best_pos = lax.top_k(cand_s, PEER_TOPK)
        expert = jnp.take_along_axis(cand_i, best_pos, axis=-1)
        g = jax.nn.softmax(best_s, axis=-1)
        act = jax.nn.gelu(jnp.einsum('thkd,td->thk', u_tab[expert], xb).astype(jnp.float32),
                          approximate=False)
        w = (g * act).astype(xb.dtype)
        return jnp.einsum('thk,thkd->td', w, v_tab[expert])

    y = lax.map(block, xp).reshape(-1, D_MODEL)[:n_tok]
    return y.reshape(lead + (D_MODEL,))


def memory_and_ffn(x, mem_k, mem_v, xattn_norm, w_mq, w_mo, ffn_norm, w_pq, sub_keys, u_tab, v_tab):
    x = x + memory_attend(rmsnorm(x, xattn_norm), mem_k, mem_v, w_mq, w_mo)
    return x + peer(rmsnorm(x, ffn_norm), w_pq, sub_keys, u_tab, v_tab)


def gather_pages(pool, page_table):
    rows = pool[page_table]
    return rows.reshape((page_table.shape[0], page_table.shape[1] * pool.shape[1]) + pool.shape[2:])


def setup_inputs(seed: int = 0) -> dict:
    key = jax.random.key(seed)
    ks = jax.random.split(key, 40)
    cnt = [0]

    def nxt():
        cnt[0] += 1
        return ks[cnt[0] - 1]

    def nrm(shape, scale=1.0):
        return jax.random.normal(nxt(), shape, jnp.float32) * scale

    def gain(shape):
        return 1.0 + 0.02 * jax.random.normal(nxt(), shape, jnp.float32)

    n_pages = PAST_LEN // PAGE_SIZE
    n_used = DEC_BATCH * n_pages
    n_pool = n_used + max(1, n_used // 4)
    page_table = jax.random.permutation(nxt(), n_pool)[:n_used].reshape(DEC_BATCH, n_pages).astype(jnp.int32)
    return {
        'x_prompt': nrm((BATCH, SEQ, D_MODEL)),
        'mem_prompt': nrm((BATCH, MEM_TOKENS, D_MODEL)),
        'x_sample': nrm((DEC_BATCH, DEC_SEQ, D_MODEL)),
        'cache_mla_latent': nrm((DEPTH, n_pool, PAGE_SIZE, KV_LORA)),
        'cache_mla_krope': nrm((DEPTH, n_pool, PAGE_SIZE, MLA_ROPE)),
        'cache_sb_k': nrm((DEPTH, n_pool, PAGE_SIZE, SB_HEADS, SB_DIM)),
        'cache_sb_v': nrm((DEPTH, n_pool, PAGE_SIZE, SB_HEADS, SB_DIM)),
        'cache_mem_k': nrm((DEPTH, DEC_BATCH, MEM_TOKENS, MEM_HEADS, MEM_DIM)),
        'cache_mem_v': nrm((DEPTH, DEC_BATCH, MEM_TOKENS, MEM_HEADS, MEM_DIM)),
        'page_table': page_table,
        'attn_norm': gain((DEPTH, D_MODEL)),
        'w_in': nrm((DEPTH, D_MODEL, IN_WIDTH), D_MODEL ** -0.5),
        'q_norm': gain((DEPTH, Q_LORA)),
        'w_uq': nrm((DEPTH, Q_LORA, MLA_HEADS * (MLA_NOPE + MLA_ROPE)), Q_LORA ** -0.5),
        'kv_norm': gain((DEPTH, KV_LORA)),
        'w_uk': nrm((DEPTH, KV_LORA, MLA_HEADS, MLA_NOPE), KV_LORA ** -0.5),
        'w_uv': nrm((DEPTH, KV_LORA, MLA_HEADS, MLA_V), KV_LORA ** -0.5),
        'g_mla_out': gain((DEPTH, MLA_WIDTH)),
        'g_sb_out': gain((DEPTH, SB_WIDTH)),
        'w_o': nrm((DEPTH, MIX_WIDTH, D_MODEL), MIX_WIDTH ** -0.5),
        'xattn_norm': gain((DEPTH, D_MODEL)),
        'mem_norm': gain((DEPTH, D_MODEL)),
        'w_mq': nrm((DEPTH, D_MODEL, MEM_WIDTH), D_MODEL ** -0.5),
        'w_mk': nrm((DEPTH, D_MODEL, MEM_WIDTH), D_MODEL ** -0.5),
        'w_mv': nrm((DEPTH, D_MODEL, MEM_WIDTH), D_MODEL ** -0.5),
        'w_mo': nrm((DEPTH, MEM_WIDTH, D_MODEL), MEM_WIDTH ** -0.5),
        'ffn_norm': gain((DEPTH, D_MODEL)),
        'w_pq': nrm((DEPTH, D_MODEL, PEER_HEADS * PEER_KEY_DIM), D_MODEL ** -0.5),
        'peer_sub_keys': nrm((DEPTH, PEER_HEADS, 2, N_KEYS, PEER_HALF), PEER_HALF ** -0.5),
        'peer_u': nrm((DEPTH, N_EXPERTS, D_MODEL), D_MODEL ** -0.5),
        'peer_v': nrm((DEPTH, N_EXPERTS, D_MODEL), PEER_HEADS ** -0.5),
        'final_norm': gain((D_MODEL,)),
    }


def reference(x_prompt, mem_prompt, x_sample, cache_mla_latent, cache_mla_krope, cache_sb_k, cache_sb_v,
              cache_mem_k, cache_mem_v, page_table, attn_norm, w_in, q_norm, w_uq, kv_norm, w_uk, w_uv,
              g_mla_out, g_sb_out, w_o, xattn_norm, mem_norm, w_mq, w_mk, w_mv, w_mo, ffn_norm, w_pq,
              peer_sub_keys, peer_u, peer_v, final_norm):
    pos_p = jnp.arange(x_prompt.shape[1])
    pos_s = PAST_LEN + jnp.arange(x_sample.shape[1])
    hp, hs = x_prompt, x_sample
    p_lat, p_kr, p_k, p_v, p_mk, p_mv = [], [], [], [], [], []
    s_lat, s_kr, s_k, s_v = [], [], [], []
    for l in range(DEPTH):
        mix_w = (w_in[l], q_norm[l], w_uq[l], kv_norm[l], w_uk[l], w_uv[l], g_mla_out[l], g_sb_out[l], w_o[l])
        tail_w = (xattn_norm[l], w_mq[l], w_mo[l], ffn_norm[l], w_pq[l], peer_sub_keys[l], peer_u[l], peer_v[l])
        d, (c, kr, k, v) = token_mixers(rmsnorm(hp, attn_norm[l]), pos_p, None, *mix_w)
        hp = hp + d
        mk, mv = memory_kv(mem_prompt, mem_norm[l], w_mk[l], w_mv[l])
        hp = memory_and_ffn(hp, mk, mv, *tail_w)
        p_lat.append(c); p_kr.append(kr); p_k.append(k); p_v.append(v); p_mk.append(mk); p_mv.append(mv)
        past = (gather_pages(cache_mla_latent[l], page_table), gather_pages(cache_mla_krope[l], page_table),
                gather_pages(cache_sb_k[l], page_table), gather_pages(cache_sb_v[l], page_table))
        d, (c, kr, k, v) = token_mixers(rmsnorm(hs, attn_norm[l]), pos_s, past, *mix_w)
        hs = hs + d
        hs = memory_and_ffn(hs, cache_mem_k[l], cache_mem_v[l], *tail_w)
        s_lat.append(c); s_kr.append(kr); s_k.append(k); s_v.append(v)
    y_prompt = rmsnorm(hp, final_norm)
    y_sample = rmsnorm(hs, final_norm)
    prompt_mla_latent = jnp.stack(p_lat)
    prompt_mla_krope = jnp.stack(p_kr)
    prompt_sb_k = jnp.stack(p_k)
    prompt_sb_v = jnp.stack(p_v)
    prompt_mem_k = jnp.stack(p_mk)
    prompt_mem_v = jnp.stack(p_mv)
    sample_mla_latent = jnp.stack(s_lat)
    sample_mla_krope = jnp.stack(s_kr)
    sample_sb_k = jnp.stack(s_k)
    sample_sb_v = jnp.stack(s_v)
    return (y_prompt, y_sample, prompt_mla_latent, prompt_mla_krope, prompt_sb_k, prompt_sb_v,
            prompt_mem_k, prompt_mem_v, sample_mla_latent, sample_mla_krope, sample_sb_k, sample_sb_v)
```

```python
import functools
import math

import jax
import jax.numpy as jnp
from jax import lax
from jax.experimental import pallas as pl
from jax.experimental.pallas import tpu as pltpu

D_MODEL = 1024
MLA_HEADS = 8
MLA_NOPE = 64
MLA_ROPE = 32
MLA_V = 64
Q_LORA = 768
KV_LORA = 256
MLA_SCALE = (MLA_NOPE + MLA_ROPE) ** -0.5
SB_HEADS = 8
SB_DIM = 64
SB_SCALE = SB_DIM ** -0.5
SB_WIDTH = SB_HEADS * SB_DIM
MLA_WIDTH = MLA_HEADS * MLA_V
MEM_HEADS = 4
MEM_DIM = 64
MEM_WIDTH = MEM_HEADS * MEM_DIM
N_KEYS = 128
PEER_HEADS = 8
PEER_HALF = 128
PEER_TOPK = 16
ROPE_THETA = 10000.0
EPS = 1e-6

LANES = 128
SUBLANES = 8
VMEM_LIMIT = 56 * 1024 * 1024
NEG_BIG = -1e30

ROPE_LO = MLA_NOPE
ROPE_MID = MLA_NOPE + MLA_ROPE // 2
ROPE_HI = MLA_NOPE + MLA_ROPE

TOKEN_BLOCK = 512
ATT_BLOCK = 256
PAGES_PER_STEP = 16
PEER_TOPK_BLOCK = 256
PEER_GATHER_BLOCK = 8

BF16 = jnp.bfloat16
F32 = jnp.float32


def _params(sem):
    return pltpu.CompilerParams(dimension_semantics=sem, vmem_limit_bytes=VMEM_LIMIT)


def _rms(x, g):
    return x * lax.rsqrt(jnp.mean(x * x, axis=-1, keepdims=True) + EPS) * g


def _dot(a, b):
    return jnp.dot(a, b, preferred_element_type=F32)


def _dot_nt(a, b):
    return lax.dot_general(a, b, (((1,), (1,)), ((), ())), preferred_element_type=F32)


def _const_spec(shape):
    nd = len(shape)
    return pl.BlockSpec(shape, lambda *_: (0,) * nd)


def _rope128(v, c, s):
    lane = lax.broadcasted_iota(jnp.int32, v.shape, 1)
    rot = jnp.where(lane < ROPE_MID, pltpu.roll(v, LANES - MLA_ROPE // 2, 1), pltpu.roll(v, MLA_ROPE // 2, 1))
    return v * c + rot * s


def _proj_kernel(x_ref, c_ref, s_ref, gan_ref, win_ref, gq_ref, wuq_ref, gkv_ref, wuk_ref, wuv_ref,
                 q_out, lat_out, kr_out, km_out, vm_out, sq_out, sk_out, sv_out, skb_out, svb_out):
    x = x_ref[...]
    h = _rms(x, gan_ref[...]).astype(BF16)
    p = _dot(h, win_ref[...])
    o_ckv = Q_LORA
    o_qs = Q_LORA + KV_LORA
    o_ks = o_qs + SB_WIDTH
    o_vs = o_ks + SB_WIDTH
    o_kr = o_vs + SB_WIDTH
    cq = p[:, :Q_LORA]
    ckv = p[:, o_ckv:o_qs]
    c = c_ref[...]
    s = s_ref[...]
    q = _dot(_rms(cq, gq_ref[...]).astype(BF16), wuq_ref[...])
    for hd in range(MLA_HEADS):
        sl = slice(hd * LANES, (hd + 1) * LANES)
        q_out[:, sl] = _rope128(q[:, sl], c, s).astype(BF16)
    lat = _rms(ckv, gkv_ref[...])
    lat_out[...] = lat
    krr = _rope128(p[:, o_kr:o_kr + LANES], c, s)
    kr_out[...] = krr[:, ROPE_LO:ROPE_HI]
    latb = lat.astype(BF16)
    kn = _dot(latb, wuk_ref[...])
    for hd in range(MLA_HEADS):
        sl = slice(hd * LANES, (hd + 1) * LANES)
        km_out[:, sl] = (kn[:, sl] + krr).astype(BF16)
    vm_out[...] = _dot(latb, wuv_ref[...]).astype(BF16)
    sq_out[...] = (p[:, o_qs:o_ks] * SB_SCALE).astype(BF16)
    ks = p[:, o_ks:o_vs]
    vs = p[:, o_vs:o_kr]
    sk_out[...] = ks
    sv_out[...] = vs
    skb_out[...] = ks.astype(BF16)
    svb_out[...] = vs.astype(BF16)


def _input_projections(x, ctab, stab, gan, win_p, gq, wuq_p, gkv, wuk_p, wuv_f):
    n = x.shape[0]
    tm = TOKEN_BLOCK
    assert n % tm == 0
    row = lambda w: pl.BlockSpec((tm, w), lambda i: (i, 0))
    out_shapes = (
        jax.ShapeDtypeStruct((n, MLA_HEADS * LANES), BF16),
        jax.ShapeDtypeStruct((n, KV_LORA), F32),
        jax.ShapeDtypeStruct((n, MLA_ROPE), F32),
        jax.ShapeDtypeStruct((n, MLA_HEADS * LANES), BF16),
        jax.ShapeDtypeStruct((n, MLA_WIDTH), BF16),
        jax.ShapeDtypeStruct((n, SB_WIDTH), BF16),
        jax.ShapeDtypeStruct((n, SB_WIDTH), F32),
        jax.ShapeDtypeStruct((n, SB_WIDTH), F32),
        jax.ShapeDtypeStruct((n, SB_WIDTH), BF16),
        jax.ShapeDtypeStruct((n, SB_WIDTH), BF16),
    )
    return pl.pallas_call(
        _proj_kernel,
        grid=(n // tm,),
        in_specs=[row(D_MODEL), row(LANES), row(LANES), _const_spec(gan.shape), _const_spec(win_p.shape),
                  _const_spec(gq.shape), _const_spec(wuq_p.shape), _const_spec(gkv.shape),
                  _const_spec(wuk_p.shape), _const_spec(wuv_f.shape)],
        out_specs=[row(s.shape[1]) for s in out_shapes],
        out_shape=out_shapes,
        compiler_params=_params(("parallel",)),
        name="input_projections",
    )(x, ctab, stab, gan, win_p, gq, wuq_p, gkv, wuk_p, wuv_f)


def _mla_prompt_kernel(q_ref, k_ref, v_ref, o_ref):
    tq = q_ref.shape[1]
    tk = tq
    qi = pl.program_id(2)
    row = lax.broadcasted_iota(jnp.int32, (tq, tk), 0)
    col = lax.broadcasted_iota(jnp.int32, (tq, tk), 1)
    outs = []
    for hd in range(2):
        qh = q_ref[0, :, hd * LANES:(hd + 1) * LANES]

        def body(kb, carry, qh=qh, hd=hd):
            m, l, acc = carry
            start = pl.multiple_of(kb * tk, tk)
            k = k_ref[0, pl.ds(start, tk), hd * LANES:(hd + 1) * LANES]
            v = v_ref[0, pl.ds(start, tk), :]
            s = _dot_nt(qh, k)
            s = jnp.where(col + kb * tk <= row + qi * tq, s, NEG_BIG)
            m_new = jnp.maximum(m, jnp.max(s, axis=1, keepdims=True))
            alpha = jnp.exp(m - m_new)
            p = jnp.exp(s - m_new)
            l = alpha * l + jnp.sum(p, axis=1, keepdims=True)
            acc = alpha * acc + _dot(p.astype(BF16), v)
            return m_new, l, acc

        init = (jnp.full((tq, 1), NEG_BIG, F32), jnp.zeros((tq, 1), F32), jnp.zeros((tq, LANES), F32))
        m, l, acc = lax.fori_loop(0, qi + 1, body, init)
        outs.append(acc / l)
    lane = lax.broadcasted_iota(jnp.int32, (tq, LANES), 1)
    o_ref[0] = jnp.where(lane < MLA_V, outs[0], outs[1])


def _mla_prompt(q, km, vm):
    b, t, _ = q.shape
    tq = min(ATT_BLOCK, t)
    assert t % tq == 0
    return pl.pallas_call(
        _mla_prompt_kernel,
        grid=(b, MLA_HEADS // 2, t // tq),
        in_specs=[pl.BlockSpec((1, tq, 2 * LANES), lambda bi, hp, i: (bi, i, hp)),
                  pl.BlockSpec((1, t, 2 * LANES), lambda bi, hp, i: (bi, 0, hp)),
                  pl.BlockSpec((1, t, LANES), lambda bi, hp, i: (bi, 0, hp))],
        out_specs=pl.BlockSpec((1, tq, LANES), lambda bi, hp, i: (bi, i, hp)),
        out_shape=jax.ShapeDtypeStruct((b, t, MLA_WIDTH), F32),
        compiler_params=_params(("parallel", "parallel", "parallel")),
        name="mla_prompt",
    )(q, km, vm)


def _suffix_matrix(tk):
    j = lax.broadcasted_iota(jnp.int32, (tk, tk), 0)
    s = lax.broadcasted_iota(jnp.int32, (tk, tk), 1)
    return jnp.where(j > s, 1.0, 0.0).astype(BF16)


def _sb_block(z, before, r, v, tri):
    rows = z.shape[0]
    sp = jnp.maximum(z, 0.0) + jnp.log1p(jnp.exp(-jnp.abs(z)))
    lk = jnp.where(before, -sp, 0.0)
    hi = lk.astype(BF16)
    lo = (lk - hi.astype(F32)).astype(BF16)
    bl = _dot(jnp.concatenate([hi, lo], axis=0), tri)
    between = bl[:rows] + bl[rows:] + r
    a = jnp.where(before, jnp.exp(z - sp + between), 0.0)
    return _dot(a.astype(BF16), v), jnp.sum(lk, axis=1, keepdims=True)


def _sb_prompt_kernel(q_ref, k_ref, v_ref, o_ref):
    tq = q_ref.shape[1]
    tk = tq
    qi = pl.program_id(2)
    row = lax.broadcasted_iota(jnp.int32, (tq, tk), 0)
    col = lax.broadcasted_iota(jnp.int32, (tq, tk), 1)
    lane = lax.broadcasted_iota(jnp.int32, (tq, LANES), 1)
    tri = _suffix_matrix(tk)
    q = q_ref[0].astype(F32)
    outs = []
    for hd in range(2):
        in_head = (lane >= hd * SB_DIM) & (lane < (hd + 1) * SB_DIM)
        qh = jnp.where(in_head, q, 0.0).astype(BF16)

        def body(i, carry, qh=qh):
            r, acc = carry
            kb = qi - i
            start = pl.multiple_of(kb * tk, tk)
            k = k_ref[0, pl.ds(start, tk), :]
            v = v_ref[0, pl.ds(start, tk), :]
            z = _dot_nt(qh, k)
            before = col + kb * tk < row + qi * tq
            av, lsum = _sb_block(z, before, r, v, tri)
            return r + lsum, acc + av

        init = (jnp.zeros((tq, 1), F32), jnp.zeros((tq, LANES), F32))
        _, acc = lax.fori_loop(0, qi + 1, body, init)
        outs.append(acc)
    o_ref[0] = jnp.where(lane < SB_DIM, outs[0], outs[1])


def _sb_prompt(q, k, v):
    b, t, _ = q.shape
    tq = min(ATT_BLOCK, t)
    assert t % tq == 0
    return pl.pallas_call(
        _sb_prompt_kernel,
        grid=(b, SB_HEADS // 2, t // tq),
        in_specs=[pl.BlockSpec((1, tq, LANES), lambda bi, hp, i: (bi, i, hp)),
                  pl.BlockSpec((1, t, LANES), lambda bi, hp, i: (bi, 0, hp)),
                  pl.BlockSpec((1, t, LANES), lambda bi, hp, i: (bi, 0, hp))],
        out_specs=pl.BlockSpec((1, tq, LANES), lambda bi, hp, i: (bi, i, hp)),
        out_shape=jax.ShapeDtypeStruct((b, t, SB_WIDTH), F32),
        compiler_params=_params(("parallel", "parallel", "parallel")),
        name="sb_prompt",
    )(q, k, v)


def _stream_pages(pt_ref, pools, bufs, sem, npp, order):
    ns = pl.num_programs(1)
    g = pl.program_id(0) * ns + pl.program_id(1)
    slot = g % 2

    def copy(a, pid, s, k):
        return pltpu.make_async_copy(pools[a].at[pid], bufs[a].at[s, k], sem.at[a, s])

    def start(g1, s):
        b1 = g1 // ns
        j1 = g1 - b1 * ns
        for k in range(npp):
            pid = pt_ref[b1, order(j1, k)]
            for a in range(len(pools)):
                copy(a, pid, s, k).start()

    @pl.when(g == 0)
    def _():
        start(g, 0)

    @pl.when(g + 1 < pl.num_programs(0) * ns)
    def _():
        start(g + 1, 1 - slot)

    for k in range(npp):
        for a in range(len(pools)):
            copy(a, 0, slot, k).wait()
    return slot


def _qlat_kernel(q_ref, wukt_ref, o_ref):
    for hd in range(MLA_HEADS):
        o_ref[:, hd * KV_LORA:(hd + 1) * KV_LORA] = _dot(
            q_ref[:, hd * LANES:(hd + 1) * LANES], wukt_ref[hd]).astype(BF16)


def _absorbed_queries(q, wukt_p):
    n = q.shape[0]
    return pl.pallas_call(
        _qlat_kernel,
        grid=(1,),
        in_specs=[_const_spec(q.shape), _const_spec(wukt_p.shape)],
        out_specs=_const_spec((n, MLA_HEADS * KV_LORA)),
        out_shape=jax.ShapeDtypeStruct((n, MLA_HEADS * KV_LORA), BF16),
        compiler_params=_params(("arbitrary",)),
        name="absorbed_queries",
    )(q, wukt_p)


def _mla_decode_kernel(pt_ref, ql_ref, qr_ref, ln_ref, kn_ref, lat_hbm, kr_hbm, o_ref,
                       lat_buf, kr_buf, sem, m_ref, l_ref, acc_ref, *, npp, page, n_new):
    j = pl.program_id(1)
    ql = ql_ref[0]
    qr = qr_ref[0]
    slot = _stream_pages(pt_ref, (lat_hbm, kr_hbm), (lat_buf, kr_buf), sem, npp, lambda jj, k: jj * npp + k)

    @pl.when(j == 0)
    def _():
        ln = jnp.concatenate([ln_ref[0].astype(BF16), jnp.zeros((page - SUBLANES, KV_LORA), BF16)], axis=0)
        kn = jnp.concatenate([kn_ref[0].astype(BF16), jnp.zeros((page - SUBLANES, MLA_ROPE), BF16)], axis=0)
        s = _dot_nt(ql, ln) + _dot_nt(qr, kn)
        tok = lax.broadcasted_iota(jnp.int32, s.shape, 0) // MLA_HEADS
        key = lax.broadcasted_iota(jnp.int32, s.shape, 1)
        s = jnp.where((key <= tok) & (key < n_new), s, NEG_BIG)
        m = jnp.max(s, axis=1, keepdims=True)
        p = jnp.exp(s - m)
        m_ref[...] = m
        l_ref[...] = jnp.sum(p, axis=1, keepdims=True)
        acc_ref[...] = _dot(p.astype(BF16), ln)

    lats = [lat_buf[slot, k].astype(BF16) for k in range(npp)]
    ss = [_dot_nt(ql, lats[k]) + _dot_nt(qr, kr_buf[slot, k].astype(BF16)) for k in range(npp)]
    m_old = m_ref[...]
    m_new = m_old
    for s in ss:
        m_new = jnp.maximum(m_new, jnp.max(s, axis=1, keepdims=True))
    alpha = jnp.exp(m_old - m_new)
    l = alpha * l_ref[...]
    acc = alpha * acc_ref[...]
    for k in range(npp):
        p = jnp.exp(ss[k] - m_new)
        l = l + jnp.sum(p, axis=1, keepdims=True)
        acc = acc + _dot(p.astype(BF16), lats[k])
    m_ref[...] = m_new
    l_ref[...] = l
    acc_ref[...] = acc

    @pl.when(j == pl.num_programs(1) - 1)
    def _():
        o_ref[0] = acc / l


def _mla_decode(page_table, qlat, qrope, lat_new, kr_new, cache_lat, cache_kr, n_new):
    nb, n_pages = page_table.shape
    npp = min(PAGES_PER_STEP, n_pages)
    assert n_pages % npp == 0
    rows = qlat.shape[1]
    page = cache_lat.shape[1]
    seq = lambda blk: pl.BlockSpec((1,) + blk, lambda b, j, pt: (b, 0, 0))
    hbm = pl.BlockSpec(memory_space=pl.ANY)
    grid_spec = pltpu.PrefetchScalarGridSpec(
        num_scalar_prefetch=1,
        grid=(nb, n_pages // npp),
        in_specs=[seq((rows, KV_LORA)), seq((rows, MLA_ROPE)), seq((SUBLANES, KV_LORA)),
                  seq((SUBLANES, MLA_ROPE)), hbm, hbm],
        out_specs=seq((rows, KV_LORA)),
        scratch_shapes=[pltpu.VMEM((2, npp, page, KV_LORA), F32), pltpu.VMEM((2, npp, page, MLA_ROPE), F32),
                        pltpu.SemaphoreType.DMA((2, 2)),
                        pltpu.VMEM((rows, 1), F32), pltpu.VMEM((rows, 1), F32), pltpu.VMEM((rows, KV_LORA), F32)],
    )
    return pl.pallas_call(
        functools.partial(_mla_decode_kernel, npp=npp, page=page, n_new=n_new),
        grid_spec=grid_spec,
        out_shape=jax.ShapeDtypeStruct((nb, rows, KV_LORA), F32),
        compiler_params=_params(("arbitrary", "arbitrary")),
        name="mla_decode",
    )(page_table, qlat, qrope, lat_new, kr_new, cache_lat, cache_kr)


def _sb_decode_kernel(pt_ref, q_ref, kn_ref, vn_ref, k_hbm, v_hbm, o_ref, k_buf, v_buf, sem, r_ref, acc_ref,
                      *, npp, page, n_pages, n_new):
    j = pl.program_id(1)
    q = q_ref[0]
    rows = q.shape[0]
    tri = _suffix_matrix(page)
    slot = _stream_pages(pt_ref, (k_hbm, v_hbm), (k_buf, v_buf), sem, npp,
                         lambda jj, k: n_pages - 1 - (jj * npp + k))

    @pl.when(j == 0)
    def _():
        pad = jnp.zeros((page - SUBLANES, SB_WIDTH), BF16)
        kn = jnp.concatenate([kn_ref[0].astype(BF16), pad], axis=0)
        vn = jnp.concatenate([vn_ref[0].astype(BF16), pad], axis=0)
        z = _dot_nt(q, kn)
        tok = lax.broadcasted_iota(jnp.int32, z.shape, 0) // SB_HEADS
        key = lax.broadcasted_iota(jnp.int32, z.shape, 1)
        before = (key < tok) & (key < n_new)
        av, lsum = _sb_block(z, before, jnp.zeros((rows, 1), F32), vn, tri)
        r_ref[...] = lsum
        acc_ref[...] = av

    r = r_ref[...]
    acc = acc_ref[...]
    every = jnp.full((rows, page), True)
    for k in range(npp):
        kp = k_buf[slot, k].astype(BF16)
        vp = v_buf[slot, k].astype(BF16)
        av, lsum = _sb_block(_dot_nt(q, kp), every, r, vp, tri)
        acc = acc + av
        r = r + lsum
    r_ref[...] = r
    acc_ref[...] = acc

    @pl.when(j == pl.num_programs(1) - 1)
    def _():
        rr = lax.broadcasted_iota(jnp.int32, acc.shape, 0) % SB_HEADS
        hh = lax.broadcasted_iota(jnp.int32, acc.shape, 1) // SB_DIM
        own = jnp.where(rr == hh, acc, 0.0)
        o_ref[0] = jnp.sum(own.reshape(rows // SB_HEADS, SB_HEADS, SB_WIDTH), axis=1)


def _sb_decode(page_table, qbd, k_new, v_new, cache_k, cache_v, n_new):
    nb, n_pages = page_table.shape
    npp = min(PAGES_PER_STEP, n_pages)
    assert n_pages % npp == 0
    rows = qbd.shape[1]
    page = cache_k.shape[1]
    seq = lambda blk: pl.BlockSpec((1,) + blk, lambda b, j, pt: (b, 0, 0))
    hbm = pl.BlockSpec(memory_space=pl.ANY)
    grid_spec = pltpu.PrefetchScalarGridSpec(
        num_scalar_prefetch=1,
        grid=(nb, n_pages // npp),
        in_specs=[seq((rows, SB_WIDTH)), seq((SUBLANES, SB_WIDTH)), seq((SUBLANES, SB_WIDTH)), hbm, hbm],
        out_specs=seq((rows // SB_HEADS, SB_WIDTH)),
        scratch_shapes=[pltpu.VMEM((2, npp, page, SB_WIDTH), F32), pltpu.VMEM((2, npp, page, SB_WIDTH), F32),
                        pltpu.SemaphoreType.DMA((2, 2)),
                        pltpu.VMEM((rows, 1), F32), pltpu.VMEM((rows, SB_WIDTH), F32)],
    )
    return pl.pallas_call(
        functools.partial(_sb_decode_kernel, npp=npp, page=page, n_pages=n_pages, n_new=n_new),
        grid_spec=grid_spec,
        out_shape=jax.ShapeDtypeStruct((nb, rows // SB_HEADS, SB_WIDTH), F32),
        compiler_params=_params(("arbitrary", "arbitrary")),
        name="sb_decode",
    )(page_table, qbd, k_new, v_new, cache_k, cache_v)


def _uv_kernel(o_ref, wuv_ref, out_ref):
    for hd in range(MLA_HEADS):
        out_ref[:, hd * MLA_V:(hd + 1) * MLA_V] = _dot(
            o_ref[:, hd * KV_LORA:(hd + 1) * KV_LORA].astype(BF16), wuv_ref[hd])


def _latent_to_values(o_lat, wuv_h):
    n = o_lat.shape[0]
    return pl.pallas_call(
        _uv_kernel,
        grid=(1,),
        in_specs=[_const_spec(o_lat.shape), _const_spec(wuv_h.shape)],
        out_specs=_const_spec((n, MLA_WIDTH)),
        out_shape=jax.ShapeDtypeStruct((n, MLA_WIDTH), F32),
        compiler_params=_params(("arbitrary",)),
        name="latent_to_values",
    )(o_lat, wuv_h)


def _mix_out_kernel(x_ref, om_ref, os_ref, gm_ref, gs_ref, wo_ref, gx_ref, wmq_ref, x1_out, qm_out):
    o = jnp.concatenate([_rms(om_ref[...], gm_ref[...]), _rms(os_ref[...], gs_ref[...])], axis=1)
    x1 = x_ref[...] + _dot(o.astype(BF16), wo_ref[...])
    x1_out[...] = x1
    hm = _rms(x1, gx_ref[...]).astype(BF16)
    qm_out[...] = (_dot(hm, wmq_ref[...]) * MEM_DIM ** -0.5).astype(BF16)


def _mix_out(x, o_mla, o_sb, gm, gs, wo, gx, wmq):
    n = x.shape[0]
    tm = TOKEN_BLOCK
    row = lambda w: pl.BlockSpec((tm, w), lambda i: (i, 0))
    return pl.pallas_call(
        _mix_out_kernel,
        grid=(n // tm,),
        in_specs=[row(D_MODEL), row(MLA_WIDTH), row(SB_WIDTH), _const_spec(gm.shape), _const_spec(gs.shape),
                  _const_spec(wo.shape), _const_spec(gx.shape), _const_spec(wmq.shape)],
        out_specs=[row(D_MODEL), row(MEM_WIDTH)],
        out_shape=(jax.ShapeDtypeStruct((n, D_MODEL), F32), jax.ShapeDtypeStruct((n, MEM_WIDTH), BF16)),
        compiler_params=_params(("parallel",)),
        name="mix_out",
    )(x, o_mla, o_sb, gm, gs, wo, gx, wmq)


def _mem_kv_kernel(m_ref, g_ref, wk_ref, wv_ref, k_out, v_out):
    mn = _rms(m_ref[...], g_ref[...]).astype(BF16)
    k_out[...] = _dot(mn, wk_ref[...])
    v_out[...] = _dot(mn, wv_ref[...])


def _mem_kv(mem, g, wk, wv):
    n = mem.shape[0]
    tm = min(TOKEN_BLOCK, n)
    assert n % tm == 0
    row = lambda w: pl.BlockSpec((tm, w), lambda i: (i, 0))
    return pl.pallas_call(
        _mem_kv_kernel,
        grid=(n // tm,),
        in_specs=[row(D_MODEL), _const_spec(g.shape), _const_spec(wk.shape), _const_spec(wv.shape)],
        out_specs=[row(MEM_WIDTH), row(MEM_WIDTH)],
        out_shape=(jax.ShapeDtypeStruct((n, MEM_WIDTH), F32),) * 2,
        compiler_params=_params(("parallel",)),
        name="mem_kv",
    )(mem, g, wk, wv)


def _mem_attend_kernel(q_ref, k_ref, v_ref, o_ref):
    q = q_ref[0].astype(F32)
    k = k_ref[0].astype(BF16)
    v = v_ref[0].astype(BF16)
    lane = lax.broadcasted_iota(jnp.int32, q.shape, 1)
    out = jnp.zeros(q.shape, F32)
    for hd in range(MEM_HEADS):
        in_head = (lane >= hd * MEM_DIM) & (lane < (hd + 1) * MEM_DIM)
        s = _dot_nt(jnp.where(in_head, q, 0.0).astype(BF16), k)
        m = jnp.max(s, axis=1, keepdims=True)
        p = jnp.exp(s - m)
        p = p / jnp.sum(p, axis=1, keepdims=True)
        out = jnp.where(in_head, _dot(p.astype(BF16), v), out)
    o_ref[0] = out


def _mem_attend(q, k, v):
    g, t, _ = q.shape
    m = k.shape[1]
    tq = min(TOKEN_BLOCK, t)
    assert t % tq == 0
    return pl.pallas_call(
        _mem_attend_kernel,
        grid=(g, t // tq),
        in_specs=[pl.BlockSpec((1, tq, MEM_WIDTH), lambda b, i: (b, i, 0)),
                  pl.BlockSpec((1, m, MEM_WIDTH), lambda b, i: (b, 0, 0)),
                  pl.BlockSpec((1, m, MEM_WIDTH), lambda b, i: (b, 0, 0))],
        out_specs=pl.BlockSpec((1, tq, MEM_WIDTH), lambda b, i: (b, i, 0)),
        out_shape=jax.ShapeDtypeStruct((g, t, MEM_WIDTH), F32),
        compiler_params=_params(("parallel", "parallel")),
        name="mem_attend",
    )(q, k, v)


def _mem_out_kernel(x1_ref, om_ref, wmo_ref, gf_ref, wpq_ref, x2_out, pq_out):
    x2 = x1_ref[...] + _dot(om_ref[...].astype(BF16), wmo_ref[...])
    x2_out[...] = x2
    pq_out[...] = _dot(_rms(x2, gf_ref[...]).astype(BF16), wpq_ref[...]).astype(BF16)


def _mem_out(x1, o_mem, wmo, gf, wpq):
    n = x1.shape[0]
    tm = TOKEN_BLOCK
    row = lambda w: pl.BlockSpec((tm, w), lambda i: (i, 0))
    return pl.pallas_call(
        _mem_out_kernel,
        grid=(n // tm,),
        in_specs=[row(D_MODEL), row(MEM_WIDTH), _const_spec(wmo.shape), _const_spec(gf.shape),
                  _const_spec(wpq.shape)],
        out_specs=[row(D_MODEL), row(wpq.shape[1])],
        out_shape=(jax.ShapeDtypeStruct((n, D_MODEL), F32), jax.ShapeDtypeStruct((n, wpq.shape[1]), BF16)),
        compiler_params=_params(("parallel",)),
        name="mem_out",
    )(x1, o_mem, wmo, gf, wpq)


def _extract_topk(vals, tie, payload, k):
    big = jnp.int32(2 ** 30)
    out_v, out_t, out_p = [], [], []
    for _ in range(k):
        m = jnp.max(vals, axis=0, keepdims=True)
        tm = jnp.min(jnp.where(vals == m, tie, big), axis=0, keepdims=True)
        sel = tie == tm
        out_v.append(m)
        out_t.append(tm)
        if payload is not None:
            out_p.append(jnp.max(jnp.where(sel, payload, -1), axis=0, keepdims=True))
        vals = jnp.where(sel, -jnp.inf, vals)
    return out_v, out_t, out_p


def _candidate_groups():
    return (
        ("b", 0, 0, 0, 8), ("b", 0, 8, 8, 16), ("b", 1, 0, 0, 8), ("b", 2, 0, 0, 5), ("b", 3, 0, 0, 4),
        ("a", 0, 8, 8, 16), ("a", 0, 0, 4, 8), ("a", 1, 0, 4, 8), ("a", 2, 0, 4, 5),
    )


def _peer_topk_kernel(pq_ref, keys_ref, idx_out, g_out):
    tb = pq_ref.shape[0]
    keyid = lax.broadcasted_iota(jnp.int32, (N_KEYS, tb), 0)
    sub = lax.broadcasted_iota(jnp.int32, (SUBLANES, tb), 0)
    for hd in range(PEER_HEADS):
        tops, topi = [], []
        for half in range(2):
            c = (hd * 2 + half) * PEER_HALF
            st = _dot_nt(keys_ref[hd * 2 + half], pq_ref[:, c:c + PEER_HALF])
            v, t, _ = _extract_topk(st, keyid, None, PEER_TOPK)
            tops.append(jnp.concatenate(v, axis=0))
            topi.append(jnp.concatenate(t, axis=0))
        cs, cp, ce = [], [], []
        for vary, fixed, first, lo, hi in _candidate_groups():
            rank = sub + first
            valid = (rank >= lo) & (rank < hi)
            if vary == "b":
                s = tops[1][first:first + SUBLANES] + tops[0][fixed:fixed + 1]
                e = topi[0][fixed:fixed + 1] * N_KEYS + topi[1][first:first + SUBLANES]
                pos = fixed * PEER_TOPK + rank
            else:
                s = tops[0][first:first + SUBLANES] + tops[1][fixed:fixed + 1]
                e = topi[0][first:first + SUBLANES] * N_KEYS + topi[1][fixed:fixed + 1]
                pos = rank * PEER_TOPK + fixed
            cs.append(jnp.where(valid, s, -jnp.inf))
            cp.append(pos)
            ce.append(e)
        bv, _, be = _extract_topk(jnp.concatenate(cs, axis=0), jnp.concatenate(cp, axis=0),
                                  jnp.concatenate(ce, axis=0), PEER_TOPK)
        best = jnp.concatenate(bv, axis=0)
        ex = jnp.exp(best - best[0:1])
        g = ex / jnp.sum(ex, axis=0, keepdims=True)
        idx_out[hd * PEER_TOPK:(hd + 1) * PEER_TOPK, :] = jnp.concatenate(be, axis=0)
        g_out[hd * PEER_TOPK:(hd + 1) * PEER_TOPK, :] = g


def _peer_topk(pq, keys):
    n = pq.shape[0]
    tb = PEER_TOPK_BLOCK
    assert n % tb == 0
    ne = PEER_HEADS * PEER_TOPK
    return pl.pallas_call(
        _peer_topk_kernel,
        grid=(n // tb,),
        in_specs=[pl.BlockSpec((tb, pq.shape[1]), lambda i: (i, 0)), _const_spec(keys.shape)],
        out_specs=[pl.BlockSpec((ne, tb), lambda i: (0, i))] * 2,
        out_shape=(jax.ShapeDtypeStruct((ne, n), jnp.int32), jax.ShapeDtypeStruct((ne, n), F32)),
        compiler_params=_params(("parallel",)),
        name="peer_topk",
    )(pq, keys)


def _gelu_exact(x):
    return 0.5 * x * (1.0 + lax.erf(x * (2.0 ** -0.5)))


def _peer_expert_kernel(idx_ref, nxt_ref, x2_ref, g_ref, gf_ref, gl_ref, uv_hbm, y_ref, buf, sem):
    tbg = x2_ref.shape[0]
    ne = idx_ref.shape[1]
    i = pl.program_id(0)
    slot = i % 2

    def issue(src_ref, s):
        def per_token(t, carry):
            def per_row(r, c2):
                e = src_ref[t, r]
                pltpu.make_async_copy(uv_hbm.at[pl.ds(e, 1), :], buf.at[s, pl.ds(t * ne + r, 1), :],
                                      sem.at[s]).start()
                return c2
            return lax.fori_loop(0, ne, per_row, carry, unroll=8)
        lax.fori_loop(0, tbg, per_token, 0)

    @pl.when(i == 0)
    def _():
        issue(idx_ref, 0)

    @pl.when(i + 1 < pl.num_programs(0))
    def _():
        issue(nxt_ref, 1 - slot)

    pltpu.make_async_copy(buf.at[slot], buf.at[slot], sem.at[slot]).wait()

    gt = g_ref[0]
    for t in range(tbg):
        x2 = x2_ref[t:t + 1, :]
        xb = _rms(x2, gf_ref[...])
        rows = pl.ds(t * ne, ne)
        acc = buf[slot, rows, 0:LANES] * xb[:, 0:LANES]
        for c in range(1, D_MODEL // LANES):
            acc = acc + buf[slot, rows, c * LANES:(c + 1) * LANES] * xb[:, c * LANES:(c + 1) * LANES]
        act = _gelu_exact(jnp.sum(acc, axis=1, keepdims=True))
        w = gt[:, t:t + 1] * act
        out = jnp.sum(w * buf[slot, rows, D_MODEL:2 * D_MODEL], axis=0, keepdims=True)
        y_ref[t:t + 1, :] = _rms(x2 + out, gl_ref[...])


def _peer_experts(idx, gates, x2, gf, gl, uv):
    n = x2.shape[0]
    tbg = PEER_GATHER_BLOCK
    assert n % tbg == 0
    nblk = n // tbg
    ne = idx.shape[1]
    smem = lambda f: pl.BlockSpec((tbg, ne), f, memory_space=pltpu.SMEM)
    return pl.pallas_call(
        _peer_expert_kernel,
        grid=(nblk,),
        in_specs=[smem(lambda i: (i, 0)), smem(lambda i: (jnp.minimum(i + 1, nblk - 1), 0)),
                  pl.BlockSpec((tbg, D_MODEL), lambda i: (i, 0)),
                  pl.BlockSpec((1, ne, tbg), lambda i: (i, 0, 0)),
                  _const_spec(gf.shape), _const_spec(gl.shape),
                  pl.BlockSpec(memory_space=pl.ANY)],
        out_specs=pl.BlockSpec((tbg, D_MODEL), lambda i: (i, 0)),
        out_shape=jax.ShapeDtypeStruct((n, D_MODEL), F32),
        scratch_shapes=[pltpu.VMEM((2, tbg * ne, 2 * D_MODEL), F32), pltpu.SemaphoreType.DMA((2,))],
        compiler_params=_params(("arbitrary",)),
        name="peer_experts",
    )(idx, idx, x2, gates, gf, gl, uv)


def _rope_tables(pos):
    half = MLA_ROPE // 2
    inv = ROPE_THETA ** (-jnp.arange(half, dtype=F32) / half)
    ang = pos.astype(F32)[:, None] * inv[None, :]
    cos, sin = jnp.cos(ang), jnp.sin(ang)
    n = pos.shape[0]
    ones = jnp.ones((n, ROPE_LO), F32)
    zeros = jnp.zeros((n, LANES - ROPE_HI), F32)
    ctab = jnp.concatenate([ones, cos, cos, zeros], axis=1)
    stab = jnp.concatenate([jnp.zeros((n, ROPE_LO), F32), -sin, sin, zeros], axis=1)
    return ctab, stab


def _head_chunks(w, width):
    k, h, _ = w.shape
    return jnp.pad(w, ((0, 0), (0, 0), (0, LANES - width))).reshape(k, h * LANES)


def kernel(x_prompt, mem_prompt, x_sample, cache_mla_latent, cache_mla_krope, cache_sb_k, cache_sb_v,
           cache_mem_k, cache_mem_v, page_table, attn_norm, w_in, q_norm, w_uq, kv_norm, w_uk, w_uv,
           g_mla_out, g_sb_out, w_o, xattn_norm, mem_norm, w_mq, w_mk, w_mv, w_mo, ffn_norm, w_pq,
           peer_sub_keys, peer_u, peer_v, final_norm):
    depth = attn_norm.shape[0]
    assert depth == 1, "single-layer trunk"
    b, t, d = x_prompt.shape
    nb, ts, _ = x_sample.shape
    n_pages = page_table.shape[1]
    page = cache_mla_latent.shape[2]
    past_len = n_pages * page
    n_p, n_s = b * t, nb * ts
    n = n_p + n_s
    assert n_s % SUBLANES == 0 and n_p % TOKEN_BLOCK == 0 and n_s % TOKEN_BLOCK == 0 and ts <= SUBLANES
    l = 0
    row2 = lambda a: a.reshape(1, -1)

    o1, o2, o3, o4, o5 = (Q_LORA, Q_LORA + KV_LORA, Q_LORA + KV_LORA + MLA_ROPE,
                          Q_LORA + KV_LORA + MLA_ROPE + SB_WIDTH, Q_LORA + KV_LORA + MLA_ROPE + 2 * SB_WIDTH)
    wi = w_in[l]
    kr_cols = jnp.pad(wi[:, o2:o3], ((0, 0), (ROPE_LO, LANES - ROPE_HI)))
    win_p = jnp.concatenate([wi[:, :o2], wi[:, o3:], kr_cols], axis=1).astype(BF16)
    wuq_p = _head_chunks((w_uq[l] * MLA_SCALE).reshape(Q_LORA, MLA_HEADS, MLA_NOPE + MLA_ROPE),
                         MLA_NOPE + MLA_ROPE).astype(BF16)
    wuk_p = _head_chunks(w_uk[l], MLA_NOPE).astype(BF16)
    wuv_f = w_uv[l].reshape(KV_LORA, MLA_WIDTH).astype(BF16)
    wukt_p = jnp.pad(jnp.transpose(w_uk[l], (1, 2, 0)),
                     ((0, 0), (0, LANES - MLA_NOPE), (0, 0))).astype(BF16)
    wuv_h = jnp.transpose(w_uv[l], (1, 0, 2)).astype(BF16)

    pos = jnp.concatenate([jnp.tile(jnp.arange(t), b), jnp.tile(past_len + jnp.arange(ts), nb)])
    ctab, stab = _rope_tables(pos)

    x_all = jnp.concatenate([x_prompt.reshape(n_p, d), x_sample.reshape(n_s, d)], axis=0)
    (q_all, lat, kr, km, vm, sq, sk, sv, skb, svb) = _input_projections(
        x_all, ctab, stab, row2(attn_norm[l]), win_p, row2(q_norm[l]), wuq_p, row2(kv_norm[l]), wuk_p, wuv_f)

    o_mla_p = _mla_prompt(q_all[:n_p].reshape(b, t, -1), km[:n_p].reshape(b, t, -1), vm[:n_p].reshape(b, t, -1))
    o_sb_p = _sb_prompt(sq[:n_p].reshape(b, t, -1), skb[:n_p].reshape(b, t, -1), svb[:n_p].reshape(b, t, -1))

    rows = ts * MLA_HEADS
    q_s = q_all[n_p:]
    qlat = _absorbed_queries(q_s, wukt_p).reshape(nb, rows, KV_LORA)
    qrope = q_s.reshape(n_s, MLA_HEADS, LANES)[:, :, ROPE_LO:ROPE_HI].reshape(nb, rows, MLA_ROPE)
    pad_new = lambda a: jnp.pad(a.reshape(nb, ts, -1), ((0, 0), (0, SUBLANES - ts), (0, 0)))
    o_lat = _mla_decode(page_table, qlat, qrope, pad_new(lat[n_p:]), pad_new(kr[n_p:]),
                        cache_mla_latent[l], cache_mla_krope[l], ts)
    o_mla_s = _latent_to_values(o_lat.reshape(n_s, MLA_HEADS * KV_LORA), wuv_h)
    eye = jnp.eye(SB_HEADS, dtype=BF16)
    qbd = (sq[n_p:].reshape(n_s, 1, SB_HEADS, SB_DIM) * eye[None, :, :, None]).reshape(nb, rows, SB_WIDTH)
    o_sb_s = _sb_decode(page_table, qbd, pad_new(sk[n_p:]), pad_new(sv[n_p:]),
                        cache_sb_k[l].reshape(-1, page, SB_WIDTH), cache_sb_v[l].reshape(-1, page, SB_WIDTH), ts)

    o_mla = jnp.concatenate([o_mla_p.reshape(n_p, -1), o_mla_s], axis=0)
    o_sb = jnp.concatenate([o_sb_p.reshape(n_p, -1), o_sb_s.reshape(n_s, -1)], axis=0)
    x1, qm = _mix_out(x_all, o_mla, o_sb, row2(g_mla_out[l]), row2(g_sb_out[l]), w_o[l].astype(BF16),
                      row2(xattn_norm[l]), w_mq[l].astype(BF16))
    m_tok = mem_prompt.shape[1]
    mk, mv = _mem_kv(mem_prompt.reshape(b * m_tok, d), row2(mem_norm[l]), w_mk[l].astype(BF16),
                     w_mv[l].astype(BF16))
    om_p = _mem_attend(qm[:n_p].reshape(b, t, MEM_WIDTH), mk.reshape(b, m_tok, MEM_WIDTH),
                       mv.reshape(b, m_tok, MEM_WIDTH))
    om_s = _mem_attend(pad_new(qm[n_p:]), cache_mem_k[l].reshape(nb, -1, MEM_WIDTH),
                       cache_mem_v[l].reshape(nb, -1, MEM_WIDTH))[:, :ts]
    o_mem = jnp.concatenate([om_p.reshape(n_p, -1), om_s.reshape(n_s, -1)], axis=0)
    x2, pq = _mem_out(x1, o_mem, w_mo[l].astype(BF16), row2(ffn_norm[l]), w_pq[l].astype(BF16))
    keys = peer_sub_keys[l].reshape(PEER_HEADS * 2, N_KEYS, PEER_HALF).astype(BF16)
    idx_t, g_t = _peer_topk(pq, keys)
    ne = PEER_HEADS * PEER_TOPK
    idx = idx_t.T
    gates = jnp.transpose(g_t.reshape(ne, n // PEER_GATHER_BLOCK, PEER_GATHER_BLOCK), (1, 0, 2))
    uv = jnp.concatenate([peer_u[l], peer_v[l]], axis=1)
    y = _peer_experts(idx, gates, x2, row2(ffn_norm[l]), row2(final_norm), uv)

    st = lambda a, shp: a.reshape((1,) + shp)
    return (y[:n_p].reshape(b, t, d), y[n_p:].reshape(nb, ts, d),
            st(lat[:n_p], (b, t, KV_LORA)), st(kr[:n_p], (b, t, MLA_ROPE)),
            st(sk[:n_p], (b, t, SB_HEADS, SB_DIM)), st(sv[:n_p], (b, t, SB_HEADS, SB_DIM)),
            st(mk, (b, m_tok, MEM_HEADS, MEM_DIM)), st(mv, (b, m_tok, MEM_HEADS, MEM_DIM)),
            st(lat[n_p:], (nb, ts, KV_LORA)), st(kr[n_p:], (nb, ts, MLA_ROPE)),
            st(sk[n_p:], (nb, ts, SB_HEADS, SB_DIM)), st(sv[n_p:], (nb, ts, SB_HEADS, SB_DIM)))
```

```python
import functools
import math

import jax
import jax.numpy as jnp
from jax import lax
from jax.experimental import pallas as pl
from jax.experimental.pallas import tpu as pltpu

D_MODEL = 1024
MLA_HEADS = 8
MLA_NOPE = 64
MLA_ROPE = 32
MLA_V = 64
Q_LORA = 768
KV_LORA = 256
MLA_SCALE = (MLA_NOPE + MLA_ROPE) ** -0.5
SB_HEADS = 8
SB_DIM = 64
SB_SCALE = SB_DIM ** -0.5
SB_WIDTH = SB_HEADS * SB_DIM
MLA_WIDTH = MLA_HEADS * MLA_V
MEM_HEADS = 4
MEM_DIM = 64
MEM_WIDTH = MEM_HEADS * MEM_DIM
N_KEYS = 128
PEER_HEADS = 8
PEER_HALF = 128
PEER_TOPK = 16
ROPE_THETA = 10000.0
EPS = 1e-6

LANES = 128
SUBLANES = 8
VMEM_LIMIT = 56 * 1024 * 1024
NEG_BIG = -1e30

ROPE_LO = MLA_NOPE
ROPE_MID = MLA_NOPE + MLA_ROPE // 2
ROPE_HI = MLA_NOPE + MLA_ROPE

TOKEN_BLOCK = 512
ATT_BLOCK = 256
PAGES_PER_STEP = 16
PEER_TOPK_BLOCK = 256
PEER_GATHER_BLOCK = 8

BF16 = jnp.bfloat16
F32 = jnp.float32


def _params(sem):
    return pltpu.CompilerParams(dimension_semantics=sem, vmem_limit_bytes=VMEM_LIMIT)


def _rms(x, g):
    return x * lax.rsqrt(jnp.mean(x * x, axis=-1, keepdims=True) + EPS) * g


def _dot(a, b):
    return jnp.dot(a, b, preferred_element_type=F32)


def _dot_nt(a, b):
    return lax.dot_general(a, b, (((1,), (1,)), ((), ())), preferred_element_type=F32)


def _const_spec(shape):
    nd = len(shape)
    return pl.BlockSpec(shape, lambda *_: (0,) * nd)


def _rope128(v, c, s):
    lane = lax.broadcasted_iota(jnp.int32, v.shape, 1)
    rot = jnp.where(lane < ROPE_MID, pltpu.roll(v, LANES - MLA_ROPE // 2, 1), pltpu.roll(v, MLA_ROPE // 2, 1))
    return v * c + rot * s


def _proj_kernel(x_ref, c_ref, s_ref, gan_ref, win_ref, gq_ref, wuq_ref, gkv_ref, wuk_ref, wuv_ref,
                 q_out, lat_out, kr_out, km_out, vm_out, sq_out, sk_out, sv_out, skb_out, svb_out):
    x = x_ref[...]
    h = _rms(x, gan_ref[...]).astype(BF16)
    p = _dot(h, win_ref[...])
    o_ckv = Q_LORA
    o_qs = Q_LORA + KV_LORA
    o_ks = o_qs + SB_WIDTH
    o_vs = o_ks + SB_WIDTH
    o_kr = o_vs + SB_WIDTH
    cq = p[:, :Q_LORA]
    ckv = p[:, o_ckv:o_qs]
    c = c_ref[...]
    s = s_ref[...]
    q = _dot(_rms(cq, gq_ref[...]).astype(BF16), wuq_ref[...])
    for hd in range(MLA_HEADS):
        sl = slice(hd * LANES, (hd + 1) * LANES)
        q_out[:, sl] = _rope128(q[:, sl], c, s).astype(BF16)
    lat = _rms(ckv, gkv_ref[...])
    lat_out[...] = lat
    krr = _rope128(p[:, o_kr:o_kr + LANES], c, s)
    kr_out[...] = krr[:, ROPE_LO:ROPE_HI]
    latb = lat.astype(BF16)
    kn = _dot(latb, wuk_ref[...])
    for hd in range(MLA_HEADS):
        sl = slice(hd * LANES, (hd + 1) * LANES)
        km_out[:, sl] = (kn[:, sl] + krr).astype(BF16)
    vm_out[...] = _dot(latb, wuv_ref[...]).astype(BF16)
    sq_out[...] = (p[:, o_qs:o_ks] * SB_SCALE).astype(BF16)
    ks = p[:, o_ks:o_vs]
    vs = p[:, o_vs:o_kr]
    sk_out[...] = ks
    sv_out[...] = vs
    skb_out[...] = ks.astype(BF16)
    svb_out[...] = vs.astype(BF16)


def _input_projections(x, ctab, stab, gan, win_p, gq, wuq_p, gkv, wuk_p, wuv_f):
    n = x.shape[0]
    tm = TOKEN_BLOCK
    assert n % tm == 0
    row = lambda w: pl.BlockSpec((tm, w), lambda i: (i, 0))
    out_shapes = (
        jax.ShapeDtypeStruct((n, MLA_HEADS * LANES), BF16),
        jax.ShapeDtypeStruct((n, KV_LORA), F32),
        jax.ShapeDtypeStruct((n, MLA_ROPE), F32),
        jax.ShapeDtypeStruct((n, MLA_HEADS * LANES), BF16),
        jax.ShapeDtypeStruct((n, MLA_WIDTH), BF16),
        jax.ShapeDtypeStruct((n, SB_WIDTH), BF16),
        jax.ShapeDtypeStruct((n, SB_WIDTH), F32),
        jax.ShapeDtypeStruct((n, SB_WIDTH), F32),
        jax.ShapeDtypeStruct((n, SB_WIDTH), BF16),
        jax.ShapeDtypeStruct((n, SB_WIDTH), BF16),
    )
    return pl.pallas_call(
        _proj_kernel,
        grid=(n // tm,),
        in_specs=[row(D_MODEL), row(LANES), row(LANES), _const_spec(gan.shape), _const_spec(win_p.shape),
                  _const_spec(gq.shape), _const_spec(wuq_p.shape), _const_spec(gkv.shape),
                  _const_spec(wuk_p.shape), _const_spec(wuv_f.shape)],
        out_specs=[row(s.shape[1]) for s in out_shapes],
        out_shape=out_shapes,
        compiler_params=_params(("parallel",)),
        name="input_projections",
    )(x, ctab, stab, gan, win_p, gq, wuq_p, gkv, wuk_p, wuv_f)


def _mla_prompt_kernel(q_ref, k_ref, v_ref, o_ref):
    tq = q_ref.shape[1]
    tk = tq
    qi = pl.program_id(2)
    row = lax.broadcasted_iota(jnp.int32, (tq, tk), 0)
    col = lax.broadcasted_iota(jnp.int32, (tq, tk), 1)
    outs = []
    for hd in range(2):
        qh = q_ref[0, :, hd * LANES:(hd + 1) * LANES]

        def body(kb, carry, qh=qh, hd=hd):
            m, l, acc = carry
            start = pl.multiple_of(kb * tk, tk)
            k = k_ref[0, pl.ds(start, tk), hd * LANES:(hd + 1) * LANES]
            v = v_ref[0, pl.ds(start, tk), :]
            s = _dot_nt(qh, k)
            s = jnp.where(col + kb * tk <= row + qi * tq, s, NEG_BIG)
            m_new = jnp.maximum(m, jnp.max(s, axis=1, keepdims=True))
            alpha = jnp.exp(m - m_new)
            p = jnp.exp(s - m_new)
            l = alpha * l + jnp.sum(p, axis=1, keepdims=True)
            acc = alpha * acc + _dot(p.astype(BF16), v)
            return m_new, l, acc

        init = (jnp.full((tq, 1), NEG_BIG, F32), jnp.zeros((tq, 1), F32), jnp.zeros((tq, LANES), F32))
        m, l, acc = lax.fori_loop(0, qi + 1, body, init)
        outs.append(acc / l)
    lane = lax.broadcasted_iota(jnp.int32, (tq, LANES), 1)
    o_ref[0] = jnp.where(lane < MLA_V, outs[0], outs[1])


def _mla_prompt(q, km, vm):
    b, t, _ = q.shape
    tq = min(ATT_BLOCK, t)
    assert t % tq == 0
    return pl.pallas_call(
        _mla_prompt_kernel,
        grid=(b, MLA_HEADS // 2, t // tq),
        in_specs=[pl.BlockSpec((1, tq, 2 * LANES), lambda bi, hp, i: (bi, i, hp)),
                  pl.BlockSpec((1, t, 2 * LANES), lambda bi, hp, i: (bi, 0, hp)),
                  pl.BlockSpec((1, t, LANES), lambda bi, hp, i: (bi, 0, hp))],
        out_specs=pl.BlockSpec((1, tq, LANES), lambda bi, hp, i: (bi, i, hp)),
        out_shape=jax.ShapeDtypeStruct((b, t, MLA_WIDTH), F32),
        compiler_params=_params(("parallel", "parallel", "parallel")),
        name="mla_prompt",
    )(q, km, vm)


def _suffix_matrix(tk):
    j = lax.broadcasted_iota(jnp.int32, (tk, tk), 0)
    s = lax.broadcasted_iota(jnp.int32, (tk, tk), 1)
    return jnp.where(j > s, 1.0, 0.0).astype(BF16)


def _sb_block(z, before, r, v, tri, v_transposed=False):
    rows = z.shape[0]
    sp = jnp.maximum(z, 0.0) + jnp.log1p(jnp.exp(-jnp.abs(z)))
    lk = -sp if before is None else jnp.where(before, -sp, 0.0)
    hi = lk.astype(BF16)
    lo = (lk - hi.astype(F32)).astype(BF16)
    bl = _dot(jnp.concatenate([hi, lo], axis=0), tri)
    between = bl[:rows] + bl[rows:] + r
    a = jnp.exp(z - sp + between)
    if before is not None:
        a = jnp.where(before, a, 0.0)
    av = _dot_nt(a.astype(BF16), v) if v_transposed else _dot(a.astype(BF16), v)
    return av, jnp.sum(lk, axis=1, keepdims=True)


def _sb_prompt_kernel(q_ref, k_ref, v_ref, o_ref):
    tq = q_ref.shape[1]
    tk = tq
    qi = pl.program_id(2)
    row = lax.broadcasted_iota(jnp.int32, (tq, tk), 0)
    col = lax.broadcasted_iota(jnp.int32, (tq, tk), 1)
    lane = lax.broadcasted_iota(jnp.int32, (tq, LANES), 1)
    tri = _suffix_matrix(tk)
    q = q_ref[0].astype(F32)
    outs = []
    for hd in range(2):
        in_head = (lane >= hd * SB_DIM) & (lane < (hd + 1) * SB_DIM)
        qh = jnp.where(in_head, q, 0.0).astype(BF16)

        def body(i, carry, qh=qh):
            r, acc = carry
            kb = qi - i
            start = pl.multiple_of(kb * tk, tk)
            k = k_ref[0, pl.ds(start, tk), :]
            v = v_ref[0, pl.ds(start, tk), :]
            z = _dot_nt(qh, k)
            before = col + kb * tk < row + qi * tq
            av, lsum = _sb_block(z, before, r, v, tri)
            return r + lsum, acc + av

        init = (jnp.zeros((tq, 1), F32), jnp.zeros((tq, LANES), F32))
        _, acc = lax.fori_loop(0, qi + 1, body, init)
        outs.append(acc)
    o_ref[0] = jnp.where(lane < SB_DIM, outs[0], outs[1])


def _sb_prompt(q, k, v):
    b, t, _ = q.shape
    tq = min(ATT_BLOCK, t)
    assert t % tq == 0
    return pl.pallas_call(
        _sb_prompt_kernel,
        grid=(b, SB_HEADS // 2, t // tq),
        in_specs=[pl.BlockSpec((1, tq, LANES), lambda bi, hp, i: (bi, i, hp)),
                  pl.BlockSpec((1, t, LANES), lambda bi, hp, i: (bi, 0, hp)),
                  pl.BlockSpec((1, t, LANES), lambda bi, hp, i: (bi, 0, hp))],
        out_specs=pl.BlockSpec((1, tq, LANES), lambda bi, hp, i: (bi, i, hp)),
        out_shape=jax.ShapeDtypeStruct((b, t, SB_WIDTH), F32),
        compiler_params=_params(("parallel", "parallel", "parallel")),
        name="sb_prompt",
    )(q, k, v)


def _stream_pages(pt_ref, pools, bufs, sem, npp, order):
    ns = pl.num_programs(1)
    g = pl.program_id(0) * ns + pl.program_id(1)
    slot = g % 2

    def copy(a, pid, s, k):
        return pltpu.make_async_copy(pools[a].at[pid], bufs[a].at[s, k], sem.at[a, s])

    def start(g1, s):
        b1 = g1 // ns
        j1 = g1 - b1 * ns
        for k in range(npp):
            pid = pt_ref[b1, order(j1, k)]
            for a in range(len(pools)):
                copy(a, pid, s, k).start()

    @pl.when(g == 0)
    def _():
        start(g, 0)

    @pl.when(g + 1 < pl.num_programs(0) * ns)
    def _():
        start(g + 1, 1 - slot)

    for k in range(npp):
        for a in range(len(pools)):
            copy(a, 0, slot, k).wait()
    return slot


def _qlat_kernel(q_ref, wukt_ref, o_ref):
    for hd in range(MLA_HEADS):
        o_ref[:, hd * KV_LORA:(hd + 1) * KV_LORA] = _dot(
            q_ref[:, hd * LANES:(hd + 1) * LANES], wukt_ref[hd]).astype(BF16)


def _absorbed_queries(q, wukt_p):
    n = q.shape[0]
    return pl.pallas_call(
        _qlat_kernel,
        grid=(1,),
        in_specs=[_const_spec(q.shape), _const_spec(wukt_p.shape)],
        out_specs=_const_spec((n, MLA_HEADS * KV_LORA)),
        out_shape=jax.ShapeDtypeStruct((n, MLA_HEADS * KV_LORA), BF16),
        compiler_params=_params(("arbitrary",)),
        name="absorbed_queries",
    )(q, wukt_p)


def _mla_decode_kernel(pt_ref, ql_ref, qr_ref, ln_ref, kn_ref, lat_hbm, kr_hbm, o_ref,
                       lat_buf, kr_buf, sem, m_ref, l_ref, acc_ref, *, npp, page, n_new):
    j = pl.program_id(1)
    ql = ql_ref[0]
    qr = qr_ref[0]
    slot = _stream_pages(pt_ref, (lat_hbm, kr_hbm), (lat_buf, kr_buf), sem, npp, lambda jj, k: jj * npp + k)

    @pl.when(j == 0)
    def _():
        ln = jnp.concatenate([ln_ref[0].astype(BF16), jnp.zeros((page - SUBLANES, KV_LORA), BF16)], axis=0)
        kn = jnp.concatenate([kn_ref[0].astype(BF16), jnp.zeros((page - SUBLANES, MLA_ROPE), BF16)], axis=0)
        s = _dot_nt(ql, ln) + _dot_nt(qr, kn)
        tok = lax.broadcasted_iota(jnp.int32, s.shape, 0) // MLA_HEADS
        key = lax.broadcasted_iota(jnp.int32, s.shape, 1)
        s = jnp.where((key <= tok) & (key < n_new), s, NEG_BIG)
        m = jnp.max(s, axis=1, keepdims=True)
        p = jnp.exp(s - m)
        m_ref[...] = m
        l_ref[...] = jnp.sum(p, axis=1, keepdims=True)
        acc_ref[...] = _dot(p.astype(BF16), ln)

    lats = [lat_buf[slot, k].astype(BF16) for k in range(npp)]
    ss = [_dot_nt(ql, lats[k]) + _dot(qr, kr_buf[slot, k].astype(BF16)) for k in range(npp)]
    m_old = m_ref[...]
    m_new = m_old
    for s in ss:
        m_new = jnp.maximum(m_new, jnp.max(s, axis=1, keepdims=True))
    alpha = jnp.exp(m_old - m_new)
    l = alpha * l_ref[...]
    acc = alpha * acc_ref[...]
    for k in range(npp):
        p = jnp.exp(ss[k] - m_new)
        l = l + jnp.sum(p, axis=1, keepdims=True)
        acc = acc + _dot(p.astype(BF16), lats[k])
    m_ref[...] = m_new
    l_ref[...] = l
    acc_ref[...] = acc

    @pl.when(j == pl.num_programs(1) - 1)
    def _():
        o_ref[0] = acc / l


def _mla_decode(page_table, qlat, qrope, lat_new, kr_new, cache_lat, cache_kr, n_new):
    nb, n_pages = page_table.shape
    npp = min(PAGES_PER_STEP, n_pages)
    assert n_pages % npp == 0
    rows = qlat.shape[1]
    page = cache_lat.shape[1]
    seq = lambda blk: pl.BlockSpec((1,) + blk, lambda b, j, pt: (b, 0, 0))
    hbm = pl.BlockSpec(memory_space=pl.ANY)
    grid_spec = pltpu.PrefetchScalarGridSpec(
        num_scalar_prefetch=1,
        grid=(nb, n_pages // npp),
        in_specs=[seq((rows, KV_LORA)), seq((rows, MLA_ROPE)), seq((SUBLANES, KV_LORA)),
                  seq((SUBLANES, MLA_ROPE)), hbm, hbm],
        out_specs=seq((rows, KV_LORA)),
        scratch_shapes=[pltpu.VMEM((2, npp, page, KV_LORA), F32), pltpu.VMEM((2, npp, MLA_ROPE, page), F32),
                        pltpu.SemaphoreType.DMA((2, 2)),
                        pltpu.VMEM((rows, 1), F32), pltpu.VMEM((rows, 1), F32), pltpu.VMEM((rows, KV_LORA), F32)],
    )
    return pl.pallas_call(
        functools.partial(_mla_decode_kernel, npp=npp, page=page, n_new=n_new),
        grid_spec=grid_spec,
        out_shape=jax.ShapeDtypeStruct((nb, rows, KV_LORA), F32),
        compiler_params=_params(("arbitrary", "arbitrary")),
        name="mla_decode",
    )(page_table, qlat, qrope, lat_new, kr_new, cache_lat, cache_kr)


def _sb_decode_kernel(pt_ref, q_ref, kn_ref, vn_ref, k_hbm, v_hbm, o_ref, k_buf, v_buf, sem, r_ref, acc_ref,
                      *, npp, page, n_pages, n_new):
    j = pl.program_id(1)
    q = q_ref[0]
    rows = q.shape[0]
    tri = _suffix_matrix(page)
    slot = _stream_pages(pt_ref, (k_hbm, v_hbm), (k_buf, v_buf), sem, npp,
                         lambda jj, k: n_pages - 1 - (jj * npp + k))

    @pl.when(j == 0)
    def _():
        pad = jnp.zeros((page - SUBLANES, SB_WIDTH), BF16)
        kn = jnp.concatenate([kn_ref[0].astype(BF16), pad], axis=0)
        vn = jnp.concatenate([vn_ref[0].astype(BF16), pad], axis=0)
        z = _dot_nt(q, kn)
        tok = lax.broadcasted_iota(jnp.int32, z.shape, 0) // SB_HEADS
        key = lax.broadcasted_iota(jnp.int32, z.shape, 1)
        before = (key < tok) & (key < n_new)
        av, lsum = _sb_block(z, before, jnp.zeros((rows, 1), F32), vn, tri)
        r_ref[...] = lsum
        acc_ref[...] = av

    r = r_ref[...]
    acc = acc_ref[...]
    tri2 = _suffix_matrix(2 * page)
    for k in range(0, npp, 2):
        kp = jnp.concatenate([k_buf[slot, k + 1], k_buf[slot, k]], axis=1).astype(BF16)
        vp = jnp.concatenate([v_buf[slot, k + 1], v_buf[slot, k]], axis=1).astype(BF16)
        av, lsum = _sb_block(_dot(q, kp), None, r, vp, tri2, v_transposed=True)
        acc = acc + av
        r = r + lsum
    r_ref[...] = r
    acc_ref[...] = acc

    @pl.when(j == pl.num_programs(1) - 1)
    def _():
        rr = lax.broadcasted_iota(jnp.int32, acc.shape, 0) % SB_HEADS
        hh = lax.broadcasted_iota(jnp.int32, acc.shape, 1) // SB_DIM
        own = jnp.where(rr == hh, acc, 0.0)
        o_ref[0] = jnp.sum(own.reshape(rows // SB_HEADS, SB_HEADS, SB_WIDTH), axis=1)


def _sb_decode(page_table, qbd, k_new, v_new, cache_k, cache_v, n_new):
    nb, n_pages = page_table.shape
    npp = min(PAGES_PER_STEP, n_pages)
    assert n_pages % npp == 0 and npp % 2 == 0
    rows = qbd.shape[1]
    page = cache_k.shape[2]
    seq = lambda blk: pl.BlockSpec((1,) + blk, lambda b, j, pt: (b, 0, 0))
    hbm = pl.BlockSpec(memory_space=pl.ANY)
    grid_spec = pltpu.PrefetchScalarGridSpec(
        num_scalar_prefetch=1,
        grid=(nb, n_pages // npp),
        in_specs=[seq((rows, SB_WIDTH)), seq((SUBLANES, SB_WIDTH)), seq((SUBLANES, SB_WIDTH)), hbm, hbm],
        out_specs=seq((rows // SB_HEADS, SB_WIDTH)),
        scratch_shapes=[pltpu.VMEM((2, npp, SB_WIDTH, page), F32), pltpu.VMEM((2, npp, SB_WIDTH, page), F32),
                        pltpu.SemaphoreType.DMA((2, 2)),
                        pltpu.VMEM((rows, 1), F32), pltpu.VMEM((rows, SB_WIDTH), F32)],
    )
    return pl.pallas_call(
        functools.partial(_sb_decode_kernel, npp=npp, page=page, n_pages=n_pages, n_new=n_new),
        grid_spec=grid_spec,
        out_shape=jax.ShapeDtypeStruct((nb, rows // SB_HEADS, SB_WIDTH), F32),
        compiler_params=_params(("arbitrary", "arbitrary")),
        name="sb_decode",
    )(page_table, qbd, k_new, v_new, cache_k, cache_v)


def _uv_kernel(o_ref, wuv_ref, out_ref):
    for hd in range(MLA_HEADS):
        out_ref[:, hd * MLA_V:(hd + 1) * MLA_V] = _dot(
            o_ref[:, hd * KV_LORA:(hd + 1) * KV_LORA].astype(BF16), wuv_ref[hd])


def _latent_to_values(o_lat, wuv_h):
    n = o_lat.shape[0]
    return pl.pallas_call(
        _uv_kernel,
        grid=(1,),
        in_specs=[_const_spec(o_lat.shape), _const_spec(wuv_h.shape)],
        out_specs=_const_spec((n, MLA_WIDTH)),
        out_shape=jax.ShapeDtypeStruct((n, MLA_WIDTH), F32),
        compiler_params=_params(("arbitrary",)),
        name="latent_to_values",
    )(o_lat, wuv_h)


def _mix_out_kernel(x_ref, om_ref, os_ref, gm_ref, gs_ref, wo_ref, gx_ref, wmq_ref, x1_out, qm_out):
    o = jnp.concatenate([_rms(om_ref[...], gm_ref[...]), _rms(os_ref[...], gs_ref[...])], axis=1)
    x1 = x_ref[...] + _dot(o.astype(BF16), wo_ref[...])
    x1_out[...] = x1
    hm = _rms(x1, gx_ref[...]).astype(BF16)
    qm_out[...] = (_dot(hm, wmq_ref[...]) * MEM_DIM ** -0.5).astype(BF16)


def _mix_out(x, o_mla, o_sb, gm, gs, wo, gx, wmq):
    n = x.shape[0]
    tm = TOKEN_BLOCK
    row = lambda w: pl.BlockSpec((tm, w), lambda i: (i, 0))
    return pl.pallas_call(
        _mix_out_kernel,
        grid=(n // tm,),
        in_specs=[row(D_MODEL), row(MLA_WIDTH), row(SB_WIDTH), _const_spec(gm.shape), _const_spec(gs.shape),
                  _const_spec(wo.shape), _const_spec(gx.shape), _const_spec(wmq.shape)],
        out_specs=[row(D_MODEL), row(MEM_WIDTH)],
        out_shape=(jax.ShapeDtypeStruct((n, D_MODEL), F32), jax.ShapeDtypeStruct((n, MEM_WIDTH), BF16)),
        compiler_params=_params(("parallel",)),
        name="mix_out",
    )(x, o_mla, o_sb, gm, gs, wo, gx, wmq)


def _mem_kv_kernel(m_ref, g_ref, wk_ref, wv_ref, k_out, v_out):
    mn = _rms(m_ref[...], g_ref[...]).astype(BF16)
    k_out[...] = _dot(mn, wk_ref[...])
    v_out[...] = _dot(mn, wv_ref[...])


def _mem_kv(mem, g, wk, wv):
    n = mem.shape[0]
    tm = min(TOKEN_BLOCK, n)
    assert n % tm == 0
    row = lambda w: pl.BlockSpec((tm, w), lambda i: (i, 0))
    return pl.pallas_call(
        _mem_kv_kernel,
        grid=(n // tm,),
        in_specs=[row(D_MODEL), _const_spec(g.shape), _const_spec(wk.shape), _const_spec(wv.shape)],
        out_specs=[row(MEM_WIDTH), row(MEM_WIDTH)],
        out_shape=(jax.ShapeDtypeStruct((n, MEM_WIDTH), F32),) * 2,
        compiler_params=_params(("parallel",)),
        name="mem_kv",
    )(mem, g, wk, wv)


def _mem_attend_kernel(q_ref, k_ref, v_ref, o_ref):
    q = q_ref[0].astype(F32)
    k = k_ref[0].astype(BF16)
    v = v_ref[0].astype(BF16)
    lane = lax.broadcasted_iota(jnp.int32, q.shape, 1)
    out = jnp.zeros(q.shape, F32)
    for hd in range(MEM_HEADS):
        in_head = (lane >= hd * MEM_DIM) & (lane < (hd + 1) * MEM_DIM)
        s = _dot_nt(jnp.where(in_head, q, 0.0).astype(BF16), k)
        m = jnp.max(s, axis=1, keepdims=True)
        p = jnp.exp(s - m)
        p = p / jnp.sum(p, axis=1, keepdims=True)
        out = jnp.where(in_head, _dot(p.astype(BF16), v), out)
    o_ref[0] = out


def _mem_attend(q, k, v):
    g, t, _ = q.shape
    m = k.shape[1]
    tq = min(TOKEN_BLOCK, t)
    assert t % tq == 0
    return pl.pallas_call(
        _mem_attend_kernel,
        grid=(g, t // tq),
        in_specs=[pl.BlockSpec((1, tq, MEM_WIDTH), lambda b, i: (b, i, 0)),
                  pl.BlockSpec((1, m, MEM_WIDTH), lambda b, i: (b, 0, 0)),
                  pl.BlockSpec((1, m, MEM_WIDTH), lambda b, i: (b, 0, 0))],
        out_specs=pl.BlockSpec((1, tq, MEM_WIDTH), lambda b, i: (b, i, 0)),
        out_shape=jax.ShapeDtypeStruct((g, t, MEM_WIDTH), F32),
        compiler_params=_params(("parallel", "parallel")),
        name="mem_attend",
    )(q, k, v)


def _mem_out_kernel(x1_ref, om_ref, wmo_ref, gf_ref, wpq_ref, x2_out, pq_out):
    x2 = x1_ref[...] + _dot(om_ref[...].astype(BF16), wmo_ref[...])
    x2_out[...] = x2
    pq_out[...] = _dot(_rms(x2, gf_ref[...]).astype(BF16), wpq_ref[...]).astype(BF16)


def _mem_out(x1, o_mem, wmo, gf, wpq):
    n = x1.shape[0]
    tm = TOKEN_BLOCK
    row = lambda w: pl.BlockSpec((tm, w), lambda i: (i, 0))
    return pl.pallas_call(
        _mem_out_kernel,
        grid=(n // tm,),
        in_specs=[row(D_MODEL), row(MEM_WIDTH), _const_spec(wmo.shape), _const_spec(gf.shape),
                  _const_spec(wpq.shape)],
        out_specs=[row(D_MODEL), row(wpq.shape[1])],
        out_shape=(jax.ShapeDtypeStruct((n, D_MODEL), F32), jax.ShapeDtypeStruct((n, wpq.shape[1]), BF16)),
        compiler_params=_params(("parallel",)),
        name="mem_out",
    )(x1, o_mem, wmo, gf, wpq)


def _extract_topk(vals, tie, payload, k):
    big = jnp.int32(2 ** 30)
    out_v, out_t, out_p = [], [], []
    for _ in range(k):
        m = jnp.max(vals, axis=0, keepdims=True)
        tm = jnp.min(jnp.where(vals == m, tie, big), axis=0, keepdims=True)
        sel = tie == tm
        out_v.append(m)
        out_t.append(tm)
        if payload is not None:
            out_p.append(jnp.max(jnp.where(sel, payload, -1), axis=0, keepdims=True))
        vals = jnp.where(sel, -jnp.inf, vals)
    return out_v, out_t, out_p


def _candidate_groups():
    return (
        ("b", 0, 0, 0, 8), ("b", 0, 8, 8, 16), ("b", 1, 0, 0, 8), ("b", 2, 0, 0, 5), ("b", 3, 0, 0, 4),
        ("a", 0, 8, 8, 16), ("a", 0, 0, 4, 8), ("a", 1, 0, 4, 8), ("a", 2, 0, 4, 5),
    )


def _peer_topk_kernel(pq_ref, keys_ref, idx_out, g_out):
    tb = pq_ref.shape[0]
    keyid = lax.broadcasted_iota(jnp.int32, (N_KEYS, tb), 0)
    sub = lax.broadcasted_iota(jnp.int32, (SUBLANES, tb), 0)
    for hd in range(PEER_HEADS):
        tops, topi = [], []
        for half in range(2):
            c = (hd * 2 + half) * PEER_HALF
            st = _dot_nt(keys_ref[hd * 2 + half], pq_ref[:, c:c + PEER_HALF])
            v, t, _ = _extract_topk(st, keyid, None, PEER_TOPK)
            tops.append(jnp.concatenate(v, axis=0))
            topi.append(jnp.concatenate(t, axis=0))
        cs, cp, ce = [], [], []
        for vary, fixed, first, lo, hi in _candidate_groups():
            rank = sub + first
            valid = (rank >= lo) & (rank < hi)
            if vary == "b":
                s = tops[1][first:first + SUBLANES] + tops[0][fixed:fixed + 1]
                e = topi[0][fixed:fixed + 1] * N_KEYS + topi[1][first:first + SUBLANES]
                pos = fixed * PEER_TOPK + rank
            else:
                s = tops[0][first:first + SUBLANES] + tops[1][fixed:fixed + 1]
                e = topi[0][first:first + SUBLANES] * N_KEYS + topi[1][fixed:fixed + 1]
                pos = rank * PEER_TOPK + fixed
            cs.append(jnp.where(valid, s, -jnp.inf))
            cp.append(pos)
            ce.append(e)
        bv, _, be = _extract_topk(jnp.concatenate(cs, axis=0), jnp.concatenate(cp, axis=0),
                                  jnp.concatenate(ce, axis=0), PEER_TOPK)
        best = jnp.concatenate(bv, axis=0)
        ex = jnp.exp(best - best[0:1])
        g = ex / jnp.sum(ex, axis=0, keepdims=True)
        idx_out[hd * PEER_TOPK:(hd + 1) * PEER_TOPK, :] = jnp.concatenate(be, axis=0)
        g_out[hd * PEER_TOPK:(hd + 1) * PEER_TOPK, :] = g


def _peer_topk(pq, keys):
    n = pq.shape[0]
    tb = PEER_TOPK_BLOCK
    assert n % tb == 0
    ne = PEER_HEADS * PEER_TOPK
    return pl.pallas_call(
        _peer_topk_kernel,
        grid=(n // tb,),
        in_specs=[pl.BlockSpec((tb, pq.shape[1]), lambda i: (i, 0)), _const_spec(keys.shape)],
        out_specs=[pl.BlockSpec((ne, tb), lambda i: (0, i))] * 2,
        out_shape=(jax.ShapeDtypeStruct((ne, n), jnp.int32), jax.ShapeDtypeStruct((ne, n), F32)),
        compiler_params=_params(("parallel",)),
        name="peer_topk",
    )(pq, keys)


def _gelu_exact(x):
    return 0.5 * x * (1.0 + lax.erf(x * (2.0 ** -0.5)))


def _peer_expert_kernel(idx0_ref, idx1_ref, idx2_ref, x2_ref, g_ref, gf_ref, gl_ref, uv_hbm, y_ref,
                        buf0, buf1, sem):
    tbg = x2_ref.shape[0] // 2
    ne = g_ref.shape[1]
    p = pl.program_id(0)

    def start_block(src_ref, buf, s):
        for k in range(tbg * ne):
            pltpu.make_async_copy(uv_hbm.at[src_ref[k]], buf.at[pl.ds(k, 1), :], sem.at[s]).start()

    def wait_block(buf, s):
        pltpu.make_async_copy(buf, buf, sem.at[s]).wait()

    def evaluate(buf, blk):
        gt = g_ref[blk]
        for t in range(tbg):
            tok = blk * tbg + t
            x2 = x2_ref[tok:tok + 1, :]
            xb = _rms(x2, gf_ref[...])
            rows = pl.ds(t * ne, ne)
            acc = buf[rows, 0:LANES] * xb[:, 0:LANES]
            for c in range(1, D_MODEL // LANES):
                acc = acc + buf[rows, c * LANES:(c + 1) * LANES] * xb[:, c * LANES:(c + 1) * LANES]
            act = _gelu_exact(jnp.sum(acc, axis=1, keepdims=True))
            w = gt[:, t:t + 1] * act
            out = jnp.sum(w * buf[rows, D_MODEL:2 * D_MODEL], axis=0, keepdims=True)
            y_ref[tok:tok + 1, :] = _rms(x2 + out, gl_ref[...])

    @pl.when(p == 0)
    def _():
        start_block(idx0_ref, buf0, 0)

    start_block(idx1_ref, buf1, 1)
    wait_block(buf0, 0)
    evaluate(buf0, 0)
    start_block(idx2_ref, buf0, 0)
    wait_block(buf1, 1)
    evaluate(buf1, 1)

    @pl.when(p == pl.num_programs(0) - 1)
    def _():
        wait_block(buf0, 0)


def _peer_experts(idx, gates, x2, gf, gl, uv):
    n = x2.shape[0]
    tbg = PEER_GATHER_BLOCK
    assert n % (2 * tbg) == 0
    nblk = n // tbg
    ne = gates.shape[1]
    smem = lambda f: pl.BlockSpec((tbg * ne,), f, memory_space=pltpu.SMEM)
    tokens = lambda: pl.BlockSpec((2 * tbg, D_MODEL), lambda p: (p, 0))
    return pl.pallas_call(
        _peer_expert_kernel,
        grid=(nblk // 2,),
        in_specs=[smem(lambda p: (2 * p,)), smem(lambda p: (2 * p + 1,)),
                  smem(lambda p: (jnp.minimum(2 * p + 2, nblk - 1),)),
                  tokens(), pl.BlockSpec((2, ne, tbg), lambda p: (p, 0, 0)),
                  _const_spec(gf.shape), _const_spec(gl.shape),
                  pl.BlockSpec(memory_space=pl.ANY)],
        out_specs=tokens(),
        out_shape=jax.ShapeDtypeStruct((n, D_MODEL), F32),
        scratch_shapes=[pltpu.VMEM((tbg * ne, 2 * D_MODEL), F32), pltpu.VMEM((tbg * ne, 2 * D_MODEL), F32),
                        pltpu.SemaphoreType.DMA((2,))],
        compiler_params=_params(("arbitrary",)),
        name="peer_experts",
    )(idx, idx, idx, x2, gates, gf, gl, uv)


def _rope_tables(pos):
    half = MLA_ROPE // 2
    inv = ROPE_THETA ** (-jnp.arange(half, dtype=F32) / half)
    ang = pos.astype(F32)[:, None] * inv[None, :]
    cos, sin = jnp.cos(ang), jnp.sin(ang)
    n = pos.shape[0]
    ones = jnp.ones((n, ROPE_LO), F32)
    zeros = jnp.zeros((n, LANES - ROPE_HI), F32)
    ctab = jnp.concatenate([ones, cos, cos, zeros], axis=1)
    stab = jnp.concatenate([jnp.zeros((n, ROPE_LO), F32), -sin, sin, zeros], axis=1)
    return ctab, stab


def _head_chunks(w, width):
    k, h, _ = w.shape
    return jnp.pad(w, ((0, 0), (0, 0), (0, LANES - width))).reshape(k, h * LANES)


def kernel(x_prompt, mem_prompt, x_sample, cache_mla_latent, cache_mla_krope, cache_sb_k, cache_sb_v,
           cache_mem_k, cache_mem_v, page_table, attn_norm, w_in, q_norm, w_uq, kv_norm, w_uk, w_uv,
           g_mla_out, g_sb_out, w_o, xattn_norm, mem_norm, w_mq, w_mk, w_mv, w_mo, ffn_norm, w_pq,
           peer_sub_keys, peer_u, peer_v, final_norm):
    depth = attn_norm.shape[0]
    assert depth == 1, "single-layer trunk"
    b, t, d = x_prompt.shape
    nb, ts, _ = x_sample.shape
    n_pages = page_table.shape[1]
    page = cache_mla_latent.shape[2]
    past_len = n_pages * page
    n_p, n_s = b * t, nb * ts
    n = n_p + n_s
    assert n_s % SUBLANES == 0 and n_p % TOKEN_BLOCK == 0 and n_s % TOKEN_BLOCK == 0 and ts <= SUBLANES
    l = 0
    row2 = lambda a: a.reshape(1, -1)

    o1, o2, o3, o4, o5 = (Q_LORA, Q_LORA + KV_LORA, Q_LORA + KV_LORA + MLA_ROPE,
                          Q_LORA + KV_LORA + MLA_ROPE + SB_WIDTH, Q_LORA + KV_LORA + MLA_ROPE + 2 * SB_WIDTH)
    wi = w_in[l]
    kr_cols = jnp.pad(wi[:, o2:o3], ((0, 0), (ROPE_LO, LANES - ROPE_HI)))
    win_p = jnp.concatenate([wi[:, :o2], wi[:, o3:], kr_cols], axis=1).astype(BF16)
    wuq_p = _head_chunks((w_uq[l] * MLA_SCALE).reshape(Q_LORA, MLA_HEADS, MLA_NOPE + MLA_ROPE),
                         MLA_NOPE + MLA_ROPE).astype(BF16)
    wuk_p = _head_chunks(w_uk[l], MLA_NOPE).astype(BF16)
    wuv_f = w_uv[l].reshape(KV_LORA, MLA_WIDTH).astype(BF16)
    wukt_p = jnp.pad(jnp.transpose(w_uk[l], (1, 2, 0)),
                     ((0, 0), (0, LANES - MLA_NOPE), (0, 0))).astype(BF16)
    wuv_h = jnp.transpose(w_uv[l], (1, 0, 2)).astype(BF16)

    pos = jnp.concatenate([jnp.tile(jnp.arange(t), b), jnp.tile(past_len + jnp.arange(ts), nb)])
    ctab, stab = _rope_tables(pos)

    x_all = jnp.concatenate([x_prompt.reshape(n_p, d), x_sample.reshape(n_s, d)], axis=0)
    (q_all, lat, kr, km, vm, sq, sk, sv, skb, svb) = _input_projections(
        x_all, ctab, stab, row2(attn_norm[l]), win_p, row2(q_norm[l]), wuq_p, row2(kv_norm[l]), wuk_p, wuv_f)

    o_mla_p = _mla_prompt(q_all[:n_p].reshape(b, t, -1), km[:n_p].reshape(b, t, -1), vm[:n_p].reshape(b, t, -1))
    o_sb_p = _sb_prompt(sq[:n_p].reshape(b, t, -1), skb[:n_p].reshape(b, t, -1), svb[:n_p].reshape(b, t, -1))

    rows = ts * MLA_HEADS
    q_s = q_all[n_p:]
    qlat = _absorbed_queries(q_s, wukt_p).reshape(nb, rows, KV_LORA)
    qrope = q_s.reshape(n_s, MLA_HEADS, LANES)[:, :, ROPE_LO:ROPE_HI].reshape(nb, rows, MLA_ROPE)
    pad_new = lambda a: jnp.pad(a.reshape(nb, ts, -1), ((0, 0), (0, SUBLANES - ts), (0, 0)))
    o_lat = _mla_decode(page_table, qlat, qrope, pad_new(lat[n_p:]), pad_new(kr[n_p:]),
                        cache_mla_latent[l], jnp.transpose(cache_mla_krope[l], (0, 2, 1)), ts)
    o_mla_s = _latent_to_values(o_lat.reshape(n_s, MLA_HEADS * KV_LORA), wuv_h)
    eye = jnp.eye(SB_HEADS, dtype=BF16)
    qbd = (sq[n_p:].reshape(n_s, 1, SB_HEADS, SB_DIM) * eye[None, :, :, None]).reshape(nb, rows, SB_WIDTH)
    feature_major = lambda c: jnp.transpose(c, (0, 2, 3, 1)).reshape(-1, SB_WIDTH, page)
    o_sb_s = _sb_decode(page_table, qbd, pad_new(sk[n_p:]), pad_new(sv[n_p:]),
                        feature_major(cache_sb_k[l]), feature_major(cache_sb_v[l]), ts)

    o_mla = jnp.concatenate([o_mla_p.reshape(n_p, -1), o_mla_s], axis=0)
    o_sb = jnp.concatenate([o_sb_p.reshape(n_p, -1), o_sb_s.reshape(n_s, -1)], axis=0)
    x1, qm = _mix_out(x_all, o_mla, o_sb, row2(g_mla_out[l]), row2(g_sb_out[l]), w_o[l].astype(BF16),
                      row2(xattn_norm[l]), w_mq[l].astype(BF16))
    m_tok = mem_prompt.shape[1]
    mk, mv = _mem_kv(mem_prompt.reshape(b * m_tok, d), row2(mem_norm[l]), w_mk[l].astype(BF16),
                     w_mv[l].astype(BF16))
    om_p = _mem_attend(qm[:n_p].reshape(b, t, MEM_WIDTH), mk.reshape(b, m_tok, MEM_WIDTH),
                       mv.reshape(b, m_tok, MEM_WIDTH))
    om_s = _mem_attend(pad_new(qm[n_p:]), cache_mem_k[l].reshape(nb, -1, MEM_WIDTH),
                       cache_mem_v[l].reshape(nb, -1, MEM_WIDTH))[:, :ts]
    o_mem = jnp.concatenate([om_p.reshape(n_p, -1), om_s.reshape(n_s, -1)], axis=0)
    x2, pq = _mem_out(x1, o_mem, w_mo[l].astype(BF16), row2(ffn_norm[l]), w_pq[l].astype(BF16))
    keys = peer_sub_keys[l].reshape(PEER_HEADS * 2, N_KEYS, PEER_HALF).astype(BF16)
    idx_t, g_t = _peer_topk(pq, keys)
    ne = PEER_HEADS * PEER_TOPK
    idx = idx_t.T.reshape(-1)
    gates = jnp.transpose(g_t.reshape(ne, n // PEER_GATHER_BLOCK, PEER_GATHER_BLOCK), (1, 0, 2))
    uv = jnp.concatenate([peer_u[l], peer_v[l]], axis=1)[:, None, :]
    y = _peer_experts(idx, gates, x2, row2(ffn_norm[l]), row2(final_norm), uv)

    st = lambda a, shp: a.reshape((1,) + shp)
    return (y[:n_p].reshape(b, t, d), y[n_p:].reshape(nb, ts, d),
            st(lat[:n_p], (b, t, KV_LORA)), st(kr[:n_p], (b, t, MLA_ROPE)),
            st(sk[:n_p], (b, t, SB_HEADS, SB_DIM)), st(sv[:n_p], (b, t, SB_HEADS, SB_DIM)),
            st(mk, (b, m_tok, MEM_HEADS, MEM_DIM)), st(mv, (b, m_tok, MEM_HEADS, MEM_DIM)),
            st(lat[n_p:], (nb, ts, KV_LORA)), st(kr[n_p:], (nb, ts, MLA_ROPE)),
            st(sk[n_p:], (nb, ts, SB_HEADS, SB_DIM)), st(sv[n_p:], (nb, ts, SB_HEADS, SB_DIM)))
```

```python
import functools
import math

import jax
import jax.numpy as jnp
from jax import lax
from jax.experimental import pallas as pl
from jax.experimental.pallas import tpu as pltpu

D_MODEL = 1024
MLA_HEADS = 8
MLA_NOPE = 64
MLA_ROPE = 32
MLA_V = 64
Q_LORA = 768
KV_LORA = 256
MLA_SCALE = (MLA_NOPE + MLA_ROPE) ** -0.5
SB_HEADS = 8
SB_DIM = 64
SB_SCALE = SB_DIM ** -0.5
SB_WIDTH = SB_HEADS * SB_DIM
MLA_WIDTH = MLA_HEADS * MLA_V
MEM_HEADS = 4
MEM_DIM = 64
MEM_WIDTH = MEM_HEADS * MEM_DIM
N_KEYS = 128
PEER_HEADS = 8
PEER_HALF = 128
PEER_TOPK = 16
ROPE_THETA = 10000.0
EPS = 1e-6

LANES = 128
SUBLANES = 8
VMEM_LIMIT = 56 * 1024 * 1024
NEG_BIG = -1e30

ROPE_LO = MLA_NOPE
ROPE_MID = MLA_NOPE + MLA_ROPE // 2
ROPE_HI = MLA_NOPE + MLA_ROPE

TOKEN_BLOCK = 512
ATT_BLOCK = 256
PAGES_PER_STEP = 16
PEER_TOPK_BLOCK = 256
PEER_GATHER_BLOCK = 8

BF16 = jnp.bfloat16
F32 = jnp.float32


def _params(sem):
    return pltpu.CompilerParams(dimension_semantics=sem, vmem_limit_bytes=VMEM_LIMIT)


def _rms(x, g):
    return x * lax.rsqrt(jnp.mean(x * x, axis=-1, keepdims=True) + EPS) * g


def _dot(a, b):
    return jnp.dot(a, b, preferred_element_type=F32)


def _dot_nt(a, b):
    return lax.dot_general(a, b, (((1,), (1,)), ((), ())), preferred_element_type=F32)


def _const_spec(shape):
    nd = len(shape)
    return pl.BlockSpec(shape, lambda *_: (0,) * nd)


def _rope128(v, c, s):
    lane = lax.broadcasted_iota(jnp.int32, v.shape, 1)
    rot = jnp.where(lane < ROPE_MID, pltpu.roll(v, LANES - MLA_ROPE // 2, 1), pltpu.roll(v, MLA_ROPE // 2, 1))
    return v * c + rot * s


def _proj_kernel(x_ref, c_ref, s_ref, gan_ref, win_ref, gq_ref, wuq_ref, gkv_ref, wuk_ref, wuv_ref,
                 q_out, lat_out, kr_out, km_out, vm_out, sq_out, sk_out, sv_out, skb_out, svb_out):
    x = x_ref[...]
    h = _rms(x, gan_ref[...]).astype(BF16)
    p = _dot(h, win_ref[...])
    o_ckv = Q_LORA
    o_qs = Q_LORA + KV_LORA
    o_ks = o_qs + SB_WIDTH
    o_vs = o_ks + SB_WIDTH
    o_kr = o_vs + SB_WIDTH
    cq = p[:, :Q_LORA]
    ckv = p[:, o_ckv:o_qs]
    c = c_ref[...]
    s = s_ref[...]
    q = _dot(_rms(cq, gq_ref[...]).astype(BF16), wuq_ref[...])
    for hd in range(MLA_HEADS):
        sl = slice(hd * LANES, (hd + 1) * LANES)
        q_out[:, sl] = _rope128(q[:, sl], c, s).astype(BF16)
    lat = _rms(ckv, gkv_ref[...])
    lat_out[...] = lat
    krr = _rope128(p[:, o_kr:o_kr + LANES], c, s)
    kr_out[...] = krr[:, ROPE_LO:ROPE_HI]
    latb = lat.astype(BF16)
    kn = _dot(latb, wuk_ref[...])
    for hd in range(MLA_HEADS):
        sl = slice(hd * LANES, (hd + 1) * LANES)
        km_out[:, sl] = (kn[:, sl] + krr).astype(BF16)
    vm_out[...] = _dot(latb, wuv_ref[...]).astype(BF16)
    sq_out[...] = (p[:, o_qs:o_ks] * SB_SCALE).astype(BF16)
    ks = p[:, o_ks:o_vs]
    vs = p[:, o_vs:o_kr]
    sk_out[...] = ks
    sv_out[...] = vs
    skb_out[...] = ks.astype(BF16)
    svb_out[...] = vs.astype(BF16)


def _input_projections(x, ctab, stab, gan, win_p, gq, wuq_p, gkv, wuk_p, wuv_f):
    n = x.shape[0]
    tm = TOKEN_BLOCK
    assert n % tm == 0
    row = lambda w: pl.BlockSpec((tm, w), lambda i: (i, 0))
    out_shapes = (
        jax.ShapeDtypeStruct((n, MLA_HEADS * LANES), BF16),
        jax.ShapeDtypeStruct((n, KV_LORA), F32),
        jax.ShapeDtypeStruct((n, MLA_ROPE), F32),
        jax.ShapeDtypeStruct((n, MLA_HEADS * LANES), BF16),
        jax.ShapeDtypeStruct((n, MLA_WIDTH), BF16),
        jax.ShapeDtypeStruct((n, SB_WIDTH), BF16),
        jax.ShapeDtypeStruct((n, SB_WIDTH), F32),
        jax.ShapeDtypeStruct((n, SB_WIDTH), F32),
        jax.ShapeDtypeStruct((n, SB_WIDTH), BF16),
        jax.ShapeDtypeStruct((n, SB_WIDTH), BF16),
    )
    return pl.pallas_call(
        _proj_kernel,
        grid=(n // tm,),
        in_specs=[row(D_MODEL), row(LANES), row(LANES), _const_spec(gan.shape), _const_spec(win_p.shape),
                  _const_spec(gq.shape), _const_spec(wuq_p.shape), _const_spec(gkv.shape),
                  _const_spec(wuk_p.shape), _const_spec(wuv_f.shape)],
        out_specs=[row(s.shape[1]) for s in out_shapes],
        out_shape=out_shapes,
        compiler_params=_params(("parallel",)),
        name="input_projections",
    )(x, ctab, stab, gan, win_p, gq, wuq_p, gkv, wuk_p, wuv_f)


def _mla_prompt_kernel(q_ref, k_ref, v_ref, o_ref):
    tq = q_ref.shape[1]
    tk = tq
    qi = pl.program_id(2)
    row = lax.broadcasted_iota(jnp.int32, (tq, tk), 0)
    col = lax.broadcasted_iota(jnp.int32, (tq, tk), 1)

    def block(kb, carry, diagonal):
        start = pl.multiple_of(kb * tk, tk)
        v = v_ref[0, pl.ds(start, tk), :]
        new = []
        for hd in range(2):
            m, l, acc = carry[hd]
            k = k_ref[0, pl.ds(start, tk), hd * LANES:(hd + 1) * LANES]
            s = _dot_nt(q_ref[0, :, hd * LANES:(hd + 1) * LANES], k)
            if diagonal:
                s = jnp.where(col <= row, s, NEG_BIG)
            m_new = jnp.maximum(m, jnp.max(s, axis=1, keepdims=True))
            alpha = jnp.exp(m - m_new)
            p = jnp.exp(s - m_new)
            l = alpha * l + jnp.sum(p, axis=1, keepdims=True)
            acc = alpha * acc + _dot(p.astype(BF16), v)
            new.append((m_new, l, acc))
        return tuple(new)

    init = (jnp.full((tq, 1), NEG_BIG, F32), jnp.zeros((tq, 1), F32), jnp.zeros((tq, LANES), F32))
    carry = lax.fori_loop(0, qi, lambda kb, c: block(kb, c, False), (init, init))
    (_, l0, acc0), (_, l1, acc1) = block(qi, carry, True)
    lane = lax.broadcasted_iota(jnp.int32, (tq, LANES), 1)
    o_ref[0] = jnp.where(lane < MLA_V, acc0 / l0, acc1 / l1)


def _mla_prompt(q, km, vm):
    b, t, _ = q.shape
    tq = min(ATT_BLOCK, t)
    assert t % tq == 0
    return pl.pallas_call(
        _mla_prompt_kernel,
        grid=(b, MLA_HEADS // 2, t // tq),
        in_specs=[pl.BlockSpec((1, tq, 2 * LANES), lambda bi, hp, i: (bi, i, hp)),
                  pl.BlockSpec((1, t, 2 * LANES), lambda bi, hp, i: (bi, 0, hp)),
                  pl.BlockSpec((1, t, LANES), lambda bi, hp, i: (bi, 0, hp))],
        out_specs=pl.BlockSpec((1, tq, LANES), lambda bi, hp, i: (bi, i, hp)),
        out_shape=jax.ShapeDtypeStruct((b, t, MLA_WIDTH), F32),
        compiler_params=_params(("parallel", "parallel", "parallel")),
        name="mla_prompt",
    )(q, km, vm)


def _suffix_matrix(tk):
    j = lax.broadcasted_iota(jnp.int32, (tk, tk), 0)
    s = lax.broadcasted_iota(jnp.int32, (tk, tk), 1)
    return jnp.where(j > s, 1.0, 0.0).astype(BF16)


def _sb_block(z, before, r, v, tri, v_transposed=False):
    rows = z.shape[0]
    sp = jnp.maximum(z, 0.0) + jnp.log1p(jnp.exp(-jnp.abs(z)))
    lk = -sp if before is None else jnp.where(before, -sp, 0.0)
    hi = lk.astype(BF16)
    lo = (lk - hi.astype(F32)).astype(BF16)
    bl = _dot(jnp.concatenate([hi, lo], axis=0), tri)
    between = bl[:rows] + bl[rows:] + r
    a = jnp.exp(z - sp + between)
    if before is not None:
        a = jnp.where(before, a, 0.0)
    av = _dot_nt(a.astype(BF16), v) if v_transposed else _dot(a.astype(BF16), v)
    return av, jnp.sum(lk, axis=1, keepdims=True)


def _sb_prompt_kernel(q_ref, k_ref, v_ref, o_ref):
    tq = q_ref.shape[1]
    tk = tq
    qi = pl.program_id(2)
    lane = lax.broadcasted_iota(jnp.int32, (tq, LANES), 1)
    tri = _suffix_matrix(tk)
    q = q_ref[0].astype(F32)
    qs = jnp.concatenate([jnp.where(lane < SB_DIM, q, 0.0), jnp.where(lane >= SB_DIM, q, 0.0)],
                         axis=0).astype(BF16)

    def block(kb, carry, before):
        r, acc = carry
        start = pl.multiple_of(kb * tk, tk)
        k = k_ref[0, pl.ds(start, tk), :]
        v = v_ref[0, pl.ds(start, tk), :]
        av, lsum = _sb_block(_dot_nt(qs, k), before, r, v, tri)
        return r + lsum, acc + av

    row = lax.broadcasted_iota(jnp.int32, (2 * tq, tk), 0) % tq
    col = lax.broadcasted_iota(jnp.int32, (2 * tq, tk), 1)
    carry = block(qi, (jnp.zeros((2 * tq, 1), F32), jnp.zeros((2 * tq, LANES), F32)), col < row)
    _, acc = lax.fori_loop(0, qi, lambda i, c: block(qi - 1 - i, c, None), carry)
    o_ref[0] = jnp.where(lane < SB_DIM, acc[:tq], acc[tq:])


def _sb_prompt(q, k, v):
    b, t, _ = q.shape
    tq = min(ATT_BLOCK, t)
    assert t % tq == 0
    return pl.pallas_call(
        _sb_prompt_kernel,
        grid=(b, SB_HEADS // 2, t // tq),
        in_specs=[pl.BlockSpec((1, tq, LANES), lambda bi, hp, i: (bi, i, hp)),
                  pl.BlockSpec((1, t, LANES), lambda bi, hp, i: (bi, 0, hp)),
                  pl.BlockSpec((1, t, LANES), lambda bi, hp, i: (bi, 0, hp))],
        out_specs=pl.BlockSpec((1, tq, LANES), lambda bi, hp, i: (bi, i, hp)),
        out_shape=jax.ShapeDtypeStruct((b, t, SB_WIDTH), F32),
        compiler_params=_params(("parallel", "parallel", "parallel")),
        name="sb_prompt",
    )(q, k, v)


def _stream_pages(pt_ref, pools, bufs, sem, npp, order):
    ns = pl.num_programs(1)
    g = pl.program_id(0) * ns + pl.program_id(1)
    slot = g % 2

    def copy(a, pid, s, k):
        return pltpu.make_async_copy(pools[a].at[pid], bufs[a].at[s, k], sem.at[a, s])

    def start(g1, s):
        b1 = g1 // ns
        j1 = g1 - b1 * ns
        for k in range(npp):
            pid = pt_ref[b1, order(j1, k)]
            for a in range(len(pools)):
                copy(a, pid, s, k).start()

    @pl.when(g == 0)
    def _():
        start(g, 0)

    @pl.when(g + 1 < pl.num_programs(0) * ns)
    def _():
        start(g + 1, 1 - slot)

    for k in range(npp):
        for a in range(len(pools)):
            copy(a, 0, slot, k).wait()
    return slot


def _qlat_kernel(q_ref, wukt_ref, o_ref):
    for hd in range(MLA_HEADS):
        o_ref[:, hd * KV_LORA:(hd + 1) * KV_LORA] = _dot(
            q_ref[:, hd * LANES:(hd + 1) * LANES], wukt_ref[hd]).astype(BF16)


def _absorbed_queries(q, wukt_p):
    n = q.shape[0]
    return pl.pallas_call(
        _qlat_kernel,
        grid=(1,),
        in_specs=[_const_spec(q.shape), _const_spec(wukt_p.shape)],
        out_specs=_const_spec((n, MLA_HEADS * KV_LORA)),
        out_shape=jax.ShapeDtypeStruct((n, MLA_HEADS * KV_LORA), BF16),
        compiler_params=_params(("arbitrary",)),
        name="absorbed_queries",
    )(q, wukt_p)


def _mla_decode_kernel(pt_ref, ql_ref, qr_ref, ln_ref, kn_ref, lat_hbm, kr_hbm, o_ref,
                       lat_buf, kr_buf, sem, m_ref, l_ref, acc_ref, *, npp, page, n_new):
    j = pl.program_id(1)
    ql = ql_ref[0]
    qr = qr_ref[0]
    slot = _stream_pages(pt_ref, (lat_hbm, kr_hbm), (lat_buf, kr_buf), sem, npp, lambda jj, k: jj * npp + k)

    @pl.when(j == 0)
    def _():
        ln = jnp.concatenate([ln_ref[0].astype(BF16), jnp.zeros((page - SUBLANES, KV_LORA), BF16)], axis=0)
        kn = jnp.concatenate([kn_ref[0].astype(BF16), jnp.zeros((page - SUBLANES, MLA_ROPE), BF16)], axis=0)
        s = _dot_nt(ql, ln) + _dot_nt(qr, kn)
        tok = lax.broadcasted_iota(jnp.int32, s.shape, 0) // MLA_HEADS
        key = lax.broadcasted_iota(jnp.int32, s.shape, 1)
        s = jnp.where((key <= tok) & (key < n_new), s, NEG_BIG)
        m = jnp.max(s, axis=1, keepdims=True)
        p = jnp.exp(s - m)
        m_ref[...] = m
        l_ref[...] = jnp.sum(p, axis=1, keepdims=True)
        acc_ref[...] = _dot(p.astype(BF16), ln)

    lats = [lat_buf[slot, k].astype(BF16) for k in range(npp)]
    ss = [_dot_nt(ql, lats[k]) + _dot(qr, kr_buf[slot, k].astype(BF16)) for k in range(npp)]
    m_old = m_ref[...]
    m_new = m_old
    for s in ss:
        m_new = jnp.maximum(m_new, jnp.max(s, axis=1, keepdims=True))
    alpha = jnp.exp(m_old - m_new)
    l = alpha * l_ref[...]
    acc = alpha * acc_ref[...]
    for k in range(npp):
        p = jnp.exp(ss[k] - m_new)
        l = l + jnp.sum(p, axis=1, keepdims=True)
        acc = acc + _dot(p.astype(BF16), lats[k])
    m_ref[...] = m_new
    l_ref[...] = l
    acc_ref[...] = acc

    @pl.when(j == pl.num_programs(1) - 1)
    def _():
        o_ref[0] = acc / l


def _mla_decode(page_table, qlat, qrope, lat_new, kr_new, cache_lat, cache_kr, n_new):
    nb, n_pages = page_table.shape
    npp = min(PAGES_PER_STEP, n_pages)
    assert n_pages % npp == 0
    rows = qlat.shape[1]
    page = cache_lat.shape[1]
    seq = lambda blk: pl.BlockSpec((1,) + blk, lambda b, j, pt: (b, 0, 0))
    hbm = pl.BlockSpec(memory_space=pl.ANY)
    grid_spec = pltpu.PrefetchScalarGridSpec(
        num_scalar_prefetch=1,
        grid=(nb, n_pages // npp),
        in_specs=[seq((rows, KV_LORA)), seq((rows, MLA_ROPE)), seq((SUBLANES, KV_LORA)),
                  seq((SUBLANES, MLA_ROPE)), hbm, hbm],
        out_specs=seq((rows, KV_LORA)),
        scratch_shapes=[pltpu.VMEM((2, npp, page, KV_LORA), F32), pltpu.VMEM((2, npp, MLA_ROPE, page), F32),
                        pltpu.SemaphoreType.DMA((2, 2)),
                        pltpu.VMEM((rows, 1), F32), pltpu.VMEM((rows, 1), F32), pltpu.VMEM((rows, KV_LORA), F32)],
    )
    return pl.pallas_call(
        functools.partial(_mla_decode_kernel, npp=npp, page=page, n_new=n_new),
        grid_spec=grid_spec,
        out_shape=jax.ShapeDtypeStruct((nb, rows, KV_LORA), F32),
        compiler_params=_params(("arbitrary", "arbitrary")),
        name="mla_decode",
    )(page_table, qlat, qrope, lat_new, kr_new, cache_lat, cache_kr)


def _sb_decode_kernel(pt_ref, q_ref, kn_ref, vn_ref, k_hbm, v_hbm, o_ref, k_buf, v_buf, sem, r_ref, acc_ref,
                      *, npp, page, n_pages, n_new):
    j = pl.program_id(1)
    q = q_ref[0]
    rows = q.shape[0]
    tri = _suffix_matrix(page)
    slot = _stream_pages(pt_ref, (k_hbm, v_hbm), (k_buf, v_buf), sem, npp,
                         lambda jj, k: n_pages - 1 - (jj * npp + k))

    @pl.when(j == 0)
    def _():
        pad = jnp.zeros((page - SUBLANES, SB_WIDTH), BF16)
        kn = jnp.concatenate([kn_ref[0].astype(BF16), pad], axis=0)
        vn = jnp.concatenate([vn_ref[0].astype(BF16), pad], axis=0)
        z = _dot_nt(q, kn)
        tok = lax.broadcasted_iota(jnp.int32, z.shape, 0) // SB_HEADS
        key = lax.broadcasted_iota(jnp.int32, z.shape, 1)
        before = (key < tok) & (key < n_new)
        av, lsum = _sb_block(z, before, jnp.zeros((rows, 1), F32), vn, tri)
        r_ref[...] = lsum
        acc_ref[...] = av

    r = r_ref[...]
    acc = acc_ref[...]
    tri2 = _suffix_matrix(2 * page)
    for k in range(0, npp, 2):
        kp = jnp.concatenate([k_buf[slot, k + 1], k_buf[slot, k]], axis=1).astype(BF16)
        vp = jnp.concatenate([v_buf[slot, k + 1], v_buf[slot, k]], axis=1).astype(BF16)
        av, lsum = _sb_block(_dot(q, kp), None, r, vp, tri2, v_transposed=True)
        acc = acc + av
        r = r + lsum
    r_ref[...] = r
    acc_ref[...] = acc

    @pl.when(j == pl.num_programs(1) - 1)
    def _():
        rr = lax.broadcasted_iota(jnp.int32, acc.shape, 0) % SB_HEADS
        hh = lax.broadcasted_iota(jnp.int32, acc.shape, 1) // SB_DIM
        own = jnp.where(rr == hh, acc, 0.0)
        o_ref[0] = jnp.sum(own.reshape(rows // SB_HEADS, SB_HEADS, SB_WIDTH), axis=1)


def _sb_decode(page_table, qbd, k_new, v_new, cache_k, cache_v, n_new):
    nb, n_pages = page_table.shape
    npp = min(PAGES_PER_STEP, n_pages)
    assert n_pages % npp == 0 and npp % 2 == 0
    rows = qbd.shape[1]
    page = cache_k.shape[2]
    seq = lambda blk: pl.BlockSpec((1,) + blk, lambda b, j, pt: (b, 0, 0))
    hbm = pl.BlockSpec(memory_space=pl.ANY)
    grid_spec = pltpu.PrefetchScalarGridSpec(
        num_scalar_prefetch=1,
        grid=(nb, n_pages // npp),
        in_specs=[seq((rows, SB_WIDTH)), seq((SUBLANES, SB_WIDTH)), seq((SUBLANES, SB_WIDTH)), hbm, hbm],
        out_specs=seq((rows // SB_HEADS, SB_WIDTH)),
        scratch_shapes=[pltpu.VMEM((2, npp, SB_WIDTH, page), F32), pltpu.VMEM((2, npp, SB_WIDTH, page), F32),
                        pltpu.SemaphoreType.DMA((2, 2)),
                        pltpu.VMEM((rows, 1), F32), pltpu.VMEM((rows, SB_WIDTH), F32)],
    )
    return pl.pallas_call(
        functools.partial(_sb_decode_kernel, npp=npp, page=page, n_pages=n_pages, n_new=n_new),
        grid_spec=grid_spec,
        out_shape=jax.ShapeDtypeStruct((nb, rows // SB_HEADS, SB_WIDTH), F32),
        compiler_params=_params(("arbitrary", "arbitrary")),
        name="sb_decode",
    )(page_table, qbd, k_new, v_new, cache_k, cache_v)


def _uv_kernel(o_ref, wuv_ref, out_ref):
    for hd in range(MLA_HEADS):
        out_ref[:, hd * MLA_V:(hd + 1) * MLA_V] = _dot(
            o_ref[:, hd * KV_LORA:(hd + 1) * KV_LORA].astype(BF16), wuv_ref[hd])


def _latent_to_values(o_lat, wuv_h):
    n = o_lat.shape[0]
    return pl.pallas_call(
        _uv_kernel,
        grid=(1,),
        in_specs=[_const_spec(o_lat.shape), _const_spec(wuv_h.shape)],
        out_specs=_const_spec((n, MLA_WIDTH)),
        out_shape=jax.ShapeDtypeStruct((n, MLA_WIDTH), F32),
        compiler_params=_params(("arbitrary",)),
        name="latent_to_values",
    )(o_lat, wuv_h)


def _mix_out_kernel(x_ref, om_ref, os_ref, gm_ref, gs_ref, wo_ref, gx_ref, wmq_ref, x1_out, qm_out):
    o = jnp.concatenate([_rms(om_ref[...], gm_ref[...]), _rms(os_ref[...], gs_ref[...])], axis=1)
    x1 = x_ref[...] + _dot(o.astype(BF16), wo_ref[...])
    x1_out[...] = x1
    hm = _rms(x1, gx_ref[...]).astype(BF16)
    qm_out[...] = (_dot(hm, wmq_ref[...]) * MEM_DIM ** -0.5).astype(BF16)


def _mix_out(x, o_mla, o_sb, gm, gs, wo, gx, wmq):
    n = x.shape[0]
    tm = TOKEN_BLOCK
    row = lambda w: pl.BlockSpec((tm, w), lambda i: (i, 0))
    return pl.pallas_call(
        _mix_out_kernel,
        grid=(n // tm,),
        in_specs=[row(D_MODEL), row(MLA_WIDTH), row(SB_WIDTH), _const_spec(gm.shape), _const_spec(gs.shape),
                  _const_spec(wo.shape), _const_spec(gx.shape), _const_spec(wmq.shape)],
        out_specs=[row(D_MODEL), row(MEM_WIDTH)],
        out_shape=(jax.ShapeDtypeStruct((n, D_MODEL), F32), jax.ShapeDtypeStruct((n, MEM_WIDTH), BF16)),
        compiler_params=_params(("parallel",)),
        name="mix_out",
    )(x, o_mla, o_sb, gm, gs, wo, gx, wmq)


def _mem_kv_kernel(m_ref, g_ref, wk_ref, wv_ref, k_out, v_out):
    mn = _rms(m_ref[...], g_ref[...]).astype(BF16)
    k_out[...] = _dot(mn, wk_ref[...])
    v_out[...] = _dot(mn, wv_ref[...])


def _mem_kv(mem, g, wk, wv):
    n = mem.shape[0]
    tm = min(TOKEN_BLOCK, n)
    assert n % tm == 0
    row = lambda w: pl.BlockSpec((tm, w), lambda i: (i, 0))
    return pl.pallas_call(
        _mem_kv_kernel,
        grid=(n // tm,),
        in_specs=[row(D_MODEL), _const_spec(g.shape), _const_spec(wk.shape), _const_spec(wv.shape)],
        out_specs=[row(MEM_WIDTH), row(MEM_WIDTH)],
        out_shape=(jax.ShapeDtypeStruct((n, MEM_WIDTH), F32),) * 2,
        compiler_params=_params(("parallel",)),
        name="mem_kv",
    )(mem, g, wk, wv)


def _mem_attend_kernel(q_ref, k_ref, v_ref, o_ref):
    q = q_ref[0].astype(F32)
    k = k_ref[0].astype(BF16)
    v = v_ref[0].astype(BF16)
    lane = lax.broadcasted_iota(jnp.int32, q.shape, 1)
    out = jnp.zeros(q.shape, F32)
    for hd in range(MEM_HEADS):
        in_head = (lane >= hd * MEM_DIM) & (lane < (hd + 1) * MEM_DIM)
        s = _dot_nt(jnp.where(in_head, q, 0.0).astype(BF16), k)
        m = jnp.max(s, axis=1, keepdims=True)
        p = jnp.exp(s - m)
        p = p / jnp.sum(p, axis=1, keepdims=True)
        out = jnp.where(in_head, _dot(p.astype(BF16), v), out)
    o_ref[0] = out


def _mem_attend(q, k, v):
    g, t, _ = q.shape
    m = k.shape[1]
    tq = min(TOKEN_BLOCK, t)
    assert t % tq == 0
    return pl.pallas_call(
        _mem_attend_kernel,
        grid=(g, t // tq),
        in_specs=[pl.BlockSpec((1, tq, MEM_WIDTH), lambda b, i: (b, i, 0)),
                  pl.BlockSpec((1, m, MEM_WIDTH), lambda b, i: (b, 0, 0)),
                  pl.BlockSpec((1, m, MEM_WIDTH), lambda b, i: (b, 0, 0))],
        out_specs=pl.BlockSpec((1, tq, MEM_WIDTH), lambda b, i: (b, i, 0)),
        out_shape=jax.ShapeDtypeStruct((g, t, MEM_WIDTH), F32),
        compiler_params=_params(("parallel", "parallel")),
        name="mem_attend",
    )(q, k, v)


def _mem_out_kernel(x1_ref, om_ref, wmo_ref, gf_ref, wpq_ref, x2_out, pq_out):
    x2 = x1_ref[...] + _dot(om_ref[...].astype(BF16), wmo_ref[...])
    x2_out[...] = x2
    pq_out[...] = _dot(_rms(x2, gf_ref[...]).astype(BF16), wpq_ref[...]).astype(BF16)


def _mem_out(x1, o_mem, wmo, gf, wpq):
    n = x1.shape[0]
    tm = TOKEN_BLOCK
    row = lambda w: pl.BlockSpec((tm, w), lambda i: (i, 0))
    return pl.pallas_call(
        _mem_out_kernel,
        grid=(n // tm,),
        in_specs=[row(D_MODEL), row(MEM_WIDTH), _const_spec(wmo.shape), _const_spec(gf.shape),
                  _const_spec(wpq.shape)],
        out_specs=[row(D_MODEL), row(wpq.shape[1])],
        out_shape=(jax.ShapeDtypeStruct((n, D_MODEL), F32), jax.ShapeDtypeStruct((n, wpq.shape[1]), BF16)),
        compiler_params=_params(("parallel",)),
        name="mem_out",
    )(x1, o_mem, wmo, gf, wpq)


def _extract_topk(vals, tie, payload, k):
    big = jnp.int32(2 ** 30)
    out_v, out_t, out_p = [], [], []
    for _ in range(k):
        m = jnp.max(vals, axis=0, keepdims=True)
        tm = jnp.min(jnp.where(vals == m, tie, big), axis=0, keepdims=True)
        sel = tie == tm
        out_v.append(m)
        out_t.append(tm)
        if payload is not None:
            out_p.append(jnp.max(jnp.where(sel, payload, -1), axis=0, keepdims=True))
        vals = jnp.where(sel, -jnp.inf, vals)
    return out_v, out_t, out_p


def _candidate_groups():
    return (
        ("b", 0, 0, 0, 8), ("b", 0, 8, 8, 16), ("b", 1, 0, 0, 8), ("b", 2, 0, 0, 5), ("b", 3, 0, 0, 4),
        ("a", 0, 8, 8, 16), ("a", 0, 0, 4, 8), ("a", 1, 0, 4, 8), ("a", 2, 0, 4, 5),
    )


def _peer_topk_kernel(pq_ref, keys_ref, idx_out, g_out):
    tb = pq_ref.shape[0]
    keyid = lax.broadcasted_iota(jnp.int32, (N_KEYS, tb), 0)
    sub = lax.broadcasted_iota(jnp.int32, (SUBLANES, tb), 0)
    for hd in range(PEER_HEADS):
        tops, topi = [], []
        for half in range(2):
            c = (hd * 2 + half) * PEER_HALF
            st = _dot_nt(keys_ref[hd * 2 + half], pq_ref[:, c:c + PEER_HALF])
            v, t, _ = _extract_topk(st, keyid, None, PEER_TOPK)
            tops.append(jnp.concatenate(v, axis=0))
            topi.append(jnp.concatenate(t, axis=0))
        cs, cp, ce = [], [], []
        for vary, fixed, first, lo, hi in _candidate_groups():
            rank = sub + first
            valid = (rank >= lo) & (rank < hi)
            if vary == "b":
                s = tops[1][first:first + SUBLANES] + tops[0][fixed:fixed + 1]
                e = topi[0][fixed:fixed + 1] * N_KEYS + topi[1][first:first + SUBLANES]
                pos = fixed * PEER_TOPK + rank
            else:
                s = tops[0][first:first + SUBLANES] + tops[1][fixed:fixed + 1]
                e = topi[0][first:first + SUBLANES] * N_KEYS + topi[1][fixed:fixed + 1]
                pos = rank * PEER_TOPK + fixed
            cs.append(jnp.where(valid, s, -jnp.inf))
            cp.append(pos)
            ce.append(e)
        bv, _, be = _extract_topk(jnp.concatenate(cs, axis=0), jnp.concatenate(cp, axis=0),
                                  jnp.concatenate(ce, axis=0), PEER_TOPK)
        best = jnp.concatenate(bv, axis=0)
        ex = jnp.exp(best - best[0:1])
        g = ex / jnp.sum(ex, axis=0, keepdims=True)
        idx_out[hd * PEER_TOPK:(hd + 1) * PEER_TOPK, :] = jnp.concatenate(be, axis=0)
        g_out[hd * PEER_TOPK:(hd + 1) * PEER_TOPK, :] = g


def _peer_topk(pq, keys):
    n = pq.shape[0]
    tb = PEER_TOPK_BLOCK
    assert n % tb == 0
    ne = PEER_HEADS * PEER_TOPK
    return pl.pallas_call(
        _peer_topk_kernel,
        grid=(n // tb,),
        in_specs=[pl.BlockSpec((tb, pq.shape[1]), lambda i: (i, 0)), _const_spec(keys.shape)],
        out_specs=[pl.BlockSpec((ne, tb), lambda i: (0, i))] * 2,
        out_shape=(jax.ShapeDtypeStruct((ne, n), jnp.int32), jax.ShapeDtypeStruct((ne, n), F32)),
        compiler_params=_params(("parallel",)),
        name="peer_topk",
    )(pq, keys)


def _gelu_exact(x):
    return 0.5 * x * (1.0 + lax.erf(x * (2.0 ** -0.5)))


def _peer_expert_kernel(idx0_ref, idx1_ref, idx2_ref, x2_ref, g_ref, gf_ref, gl_ref, uv_hbm, y_ref,
                        buf0, buf1, sem):
    tbg = x2_ref.shape[0] // 2
    ne = g_ref.shape[1]
    p = pl.program_id(0)

    def start_block(src_ref, buf, s):
        for k in range(tbg * ne):
            pltpu.make_async_copy(uv_hbm.at[src_ref[k]], buf.at[pl.ds(k, 1), :], sem.at[s]).start()

    def wait_block(buf, s):
        pltpu.make_async_copy(buf, buf, sem.at[s]).wait()

    def evaluate(buf, blk):
        gt = g_ref[blk]
        for t in range(tbg):
            tok = blk * tbg + t
            x2 = x2_ref[tok:tok + 1, :]
            xb = _rms(x2, gf_ref[...])
            rows = pl.ds(t * ne, ne)
            acc = buf[rows, 0:LANES] * xb[:, 0:LANES]
            for c in range(1, D_MODEL // LANES):
                acc = acc + buf[rows, c * LANES:(c + 1) * LANES] * xb[:, c * LANES:(c + 1) * LANES]
            act = _gelu_exact(jnp.sum(acc, axis=1, keepdims=True))
            w = gt[:, t:t + 1] * act
            out = jnp.sum(w * buf[rows, D_MODEL:2 * D_MODEL], axis=0, keepdims=True)
            y_ref[tok:tok + 1, :] = _rms(x2 + out, gl_ref[...])

    @pl.when(p == 0)
    def _():
        start_block(idx0_ref, buf0, 0)

    start_block(idx1_ref, buf1, 1)
    wait_block(buf0, 0)
    evaluate(buf0, 0)
    start_block(idx2_ref, buf0, 0)
    wait_block(buf1, 1)
    evaluate(buf1, 1)

    @pl.when(p == pl.num_programs(0) - 1)
    def _():
        wait_block(buf0, 0)


def _peer_experts(idx, gates, x2, gf, gl, uv):
    n = x2.shape[0]
    tbg = PEER_GATHER_BLOCK
    assert n % (2 * tbg) == 0
    nblk = n // tbg
    ne = gates.shape[1]
    smem = lambda f: pl.BlockSpec((tbg * ne,), f, memory_space=pltpu.SMEM)
    tokens = lambda: pl.BlockSpec((2 * tbg, D_MODEL), lambda p: (p, 0))
    return pl.pallas_call(
        _peer_expert_kernel,
        grid=(nblk // 2,),
        in_specs=[smem(lambda p: (2 * p,)), smem(lambda p: (2 * p + 1,)),
                  smem(lambda p: (jnp.minimum(2 * p + 2, nblk - 1),)),
                  tokens(), pl.BlockSpec((2, ne, tbg), lambda p: (p, 0, 0)),
                  _const_spec(gf.shape), _const_spec(gl.shape),
                  pl.BlockSpec(memory_space=pl.ANY)],
        out_specs=tokens(),
        out_shape=jax.ShapeDtypeStruct((n, D_MODEL), F32),
        scratch_shapes=[pltpu.VMEM((tbg * ne, 2 * D_MODEL), F32), pltpu.VMEM((tbg * ne, 2 * D_MODEL), F32),
                        pltpu.SemaphoreType.DMA((2,))],
        compiler_params=_params(("arbitrary",)),
        name="peer_experts",
    )(idx, idx, idx, x2, gates, gf, gl, uv)


def _rope_tables(pos):
    half = MLA_ROPE // 2
    inv = ROPE_THETA ** (-jnp.arange(half, dtype=F32) / half)
    ang = pos.astype(F32)[:, None] * inv[None, :]
    cos, sin = jnp.cos(ang), jnp.sin(ang)
    n = pos.shape[0]
    ones = jnp.ones((n, ROPE_LO), F32)
    zeros = jnp.zeros((n, LANES - ROPE_HI), F32)
    ctab = jnp.concatenate([ones, cos, cos, zeros], axis=1)
    stab = jnp.concatenate([jnp.zeros((n, ROPE_LO), F32), -sin, sin, zeros], axis=1)
    return ctab, stab


def _head_chunks(w, width):
    k, h, _ = w.shape
    return jnp.pad(w, ((0, 0), (0, 0), (0, LANES - width))).reshape(k, h * LANES)


def kernel(x_prompt, mem_prompt, x_sample, cache_mla_latent, cache_mla_krope, cache_sb_k, cache_sb_v,
           cache_mem_k, cache_mem_v, page_table, attn_norm, w_in, q_norm, w_uq, kv_norm, w_uk, w_uv,
           g_mla_out, g_sb_out, w_o, xattn_norm, mem_norm, w_mq, w_mk, w_mv, w_mo, ffn_norm, w_pq,
           peer_sub_keys, peer_u, peer_v, final_norm):
    depth = attn_norm.shape[0]
    assert depth == 1, "single-layer trunk"
    b, t, d = x_prompt.shape
    nb, ts, _ = x_sample.shape
    n_pages = page_table.shape[1]
    page = cache_mla_latent.shape[2]
    past_len = n_pages * page
    n_p, n_s = b * t, nb * ts
    n = n_p + n_s
    assert n_s % SUBLANES == 0 and n_p % TOKEN_BLOCK == 0 and n_s % TOKEN_BLOCK == 0 and ts <= SUBLANES
    l = 0
    row2 = lambda a: a.reshape(1, -1)

    o1, o2, o3, o4, o5 = (Q_LORA, Q_LORA + KV_LORA, Q_LORA + KV_LORA + MLA_ROPE,
                          Q_LORA + KV_LORA + MLA_ROPE + SB_WIDTH, Q_LORA + KV_LORA + MLA_ROPE + 2 * SB_WIDTH)
    wi = w_in[l]
    kr_cols = jnp.pad(wi[:, o2:o3], ((0, 0), (ROPE_LO, LANES - ROPE_HI)))
    win_p = jnp.concatenate([wi[:, :o2], wi[:, o3:], kr_cols], axis=1).astype(BF16)
    wuq_p = _head_chunks((w_uq[l] * MLA_SCALE).reshape(Q_LORA, MLA_HEADS, MLA_NOPE + MLA_ROPE),
                         MLA_NOPE + MLA_ROPE).astype(BF16)
    wuk_p = _head_chunks(w_uk[l], MLA_NOPE).astype(BF16)
    wuv_f = w_uv[l].reshape(KV_LORA, MLA_WIDTH).astype(BF16)
    wukt_p = jnp.pad(jnp.transpose(w_uk[l], (1, 2, 0)),
                     ((0, 0), (0, LANES - MLA_NOPE), (0, 0))).astype(BF16)
    wuv_h = jnp.transpose(w_uv[l], (1, 0, 2)).astype(BF16)

    pos = jnp.concatenate([jnp.tile(jnp.arange(t), b), jnp.tile(past_len + jnp.arange(ts), nb)])
    ctab, stab = _rope_tables(pos)

    x_all = jnp.concatenate([x_prompt.reshape(n_p, d), x_sample.reshape(n_s, d)], axis=0)
    (q_all, lat, kr, km, vm, sq, sk, sv, skb, svb) = _input_projections(
        x_all, ctab, stab, row2(attn_norm[l]), win_p, row2(q_norm[l]), wuq_p, row2(kv_norm[l]), wuk_p, wuv_f)

    o_mla_p = _mla_prompt(q_all[:n_p].reshape(b, t, -1), km[:n_p].reshape(b, t, -1), vm[:n_p].reshape(b, t, -1))
    o_sb_p = _sb_prompt(sq[:n_p].reshape(b, t, -1), skb[:n_p].reshape(b, t, -1), svb[:n_p].reshape(b, t, -1))

    rows = ts * MLA_HEADS
    q_s = q_all[n_p:]
    qlat = _absorbed_queries(q_s, wukt_p).reshape(nb, rows, KV_LORA)
    qrope = q_s.reshape(n_s, MLA_HEADS, LANES)[:, :, ROPE_LO:ROPE_HI].reshape(nb, rows, MLA_ROPE)
    pad_new = lambda a: jnp.pad(a.reshape(nb, ts, -1), ((0, 0), (0, SUBLANES - ts), (0, 0)))
    o_lat = _mla_decode(page_table, qlat, qrope, pad_new(lat[n_p:]), pad_new(kr[n_p:]),
                        cache_mla_latent[l], jnp.transpose(cache_mla_krope[l], (0, 2, 1)), ts)
    o_mla_s = _latent_to_values(o_lat.reshape(n_s, MLA_HEADS * KV_LORA), wuv_h)
    eye = jnp.eye(SB_HEADS, dtype=BF16)
    qbd = (sq[n_p:].reshape(n_s, 1, SB_HEADS, SB_DIM) * eye[None, :, :, None]).reshape(nb, rows, SB_WIDTH)
    feature_major = lambda c: jnp.transpose(c, (0, 2, 3, 1)).reshape(-1, SB_WIDTH, page)
    o_sb_s = _sb_decode(page_table, qbd, pad_new(sk[n_p:]), pad_new(sv[n_p:]),
                        feature_major(cache_sb_k[l]), feature_major(cache_sb_v[l]), ts)

    o_mla = jnp.concatenate([o_mla_p.reshape(n_p, -1), o_mla_s], axis=0)
    o_sb = jnp.concatenate([o_sb_p.reshape(n_p, -1), o_sb_s.reshape(n_s, -1)], axis=0)
    x1, qm = _mix_out(x_all, o_mla, o_sb, row2(g_mla_out[l]), row2(g_sb_out[l]), w_o[l].astype(BF16),
                      row2(xattn_norm[l]), w_mq[l].astype(BF16))
    m_tok = mem_prompt.shape[1]
    mk, mv = _mem_kv(mem_prompt.reshape(b * m_tok, d), row2(mem_norm[l]), w_mk[l].astype(BF16),
                     w_mv[l].astype(BF16))
    om_p = _mem_attend(qm[:n_p].reshape(b, t, MEM_WIDTH), mk.reshape(b, m_tok, MEM_WIDTH),
                       mv.reshape(b, m_tok, MEM_WIDTH))
    om_s = _mem_attend(pad_new(qm[n_p:]), cache_mem_k[l].reshape(nb, -1, MEM_WIDTH),
                       cache_mem_v[l].reshape(nb, -1, MEM_WIDTH))[:, :ts]
    o_mem = jnp.concatenate([om_p.reshape(n_p, -1), om_s.reshape(n_s, -1)], axis=0)
    x2, pq = _mem_out(x1, o_mem, w_mo[l].astype(BF16), row2(ffn_norm[l]), w_pq[l].astype(BF16))
    keys = peer_sub_keys[l].reshape(PEER_HEADS * 2, N_KEYS, PEER_HALF).astype(BF16)
    idx_t, g_t = _peer_topk(pq, keys)
    ne = PEER_HEADS * PEER_TOPK
    idx = idx_t.T.reshape(-1)
    gates = jnp.transpose(g_t.reshape(ne, n // PEER_GATHER_BLOCK, PEER_GATHER_BLOCK), (1, 0, 2))
    uv = jnp.concatenate([peer_u[l], peer_v[l]], axis=1)[:, None, :]
    y = _peer_experts(idx, gates, x2, row2(ffn_norm[l]), row2(final_norm), uv)

    st = lambda a, shp: a.reshape((1,) + shp)
    return (y[:n_p].reshape(b, t, d), y[n_p:].reshape(nb, ts, d),
            st(lat[:n_p], (b, t, KV_LORA)), st(kr[:n_p], (b, t, MLA_ROPE)),
            st(sk[:n_p], (b, t, SB_HEADS, SB_DIM)), st(sv[:n_p], (b, t, SB_HEADS, SB_DIM)),
            st(mk, (b, m_tok, MEM_HEADS, MEM_DIM)), st(mv, (b, m_tok, MEM_HEADS, MEM_DIM)),
            st(lat[n_p:], (nb, ts, KV_LORA)), st(kr[n_p:], (nb, ts, MLA_ROPE)),
            st(sk[n_p:], (nb, ts, SB_HEADS, SB_DIM)), st(sv[n_p:], (nb, ts, SB_HEADS, SB_DIM)))
```
